```python
import math
import jax
import jax.numpy as jnp
from jax import lax
import numpy as np

D_MODEL = 1024
BATCH = 8
SEQ = 2048
DEPTH = 4
DEC_BATCH = 128
DEC_SEQ = 8
PAST_LEN = 16384
PAGE_SIZE = 128

RW_HEADS = 6
RW_HD = 64
RW_W = RW_HEADS * RW_HD
RW_DECAY_LORA = 64
RW_AAA_LORA = 64
RW_GATE_LORA = 128
RW_PROJ = 3 * RW_W + RW_DECAY_LORA + RW_AAA_LORA + RW_GATE_LORA
RW_GN_EPS = 64e-5
GLA_HEADS = 4
GLA_DK = 48
GLA_DV = 96
GLA_QK = GLA_HEADS * GLA_DK
GLA_W = GLA_HEADS * GLA_DV
GLA_GATE_LORA = 16
GLA_TAU = 16.0
GLA_CHUNK = 64
GLA_PROJ = 2 * GLA_QK + 2 * GLA_W + GLA_GATE_LORA
ML_HEADS = 4
ML_HD = 64
ML_W = ML_HEADS * ML_HD
ML_CONV = 4
ML_CHUNK = 64
ML_PROJ = 4 * ML_W + 2 * ML_HEADS
D_MIX = RW_W + GLA_W + ML_W
D_IN = RW_PROJ + GLA_PROJ + ML_PROJ
D_FF = 2752
ALPHA = (2 * DEPTH) ** 0.25
BETA = (8 * DEPTH) ** -0.25
LN_EPS = 1e-5
NORM_EPS = 1e-6

kernel_name = 'hybrid_rwkv7_gla_mlstm_decode_step'


def _f32(t):
    return t.astype(jnp.float32)


def _split(t, sizes):
    return jnp.split(t, np.cumsum(sizes)[:-1].tolist(), axis=-1)


def _standardize(t, eps):
    t = _f32(t)
    mu = jnp.mean(t, -1, keepdims=True)
    var = jnp.mean(jnp.square(t - mu), -1, keepdims=True)
    return (t - mu) * lax.rsqrt(var + eps)


def _layer_norm(t, g, b):
    return (_standardize(t, LN_EPS) * g + b).astype(t.dtype)


def _swiglu(h, wg, wu, wd):
    return (jax.nn.silu(h @ wg) * (h @ wu)) @ wd


def _rwkv7_mix(p, shift0, S0, mu, w0, w2, a0, a2, g2, k_k, k_a, r_k, gn_g, gn_b):
    B, T, _ = p.shape
    prev = jnp.concatenate([shift0[:, None].astype(p.dtype), p[:, :-1]], axis=1)
    xs = p + (prev - p) * mu
    r, k, v, wd, ad, gd = _split(xs, [RW_W, RW_W, RW_W, RW_DECAY_LORA, RW_AAA_LORA, RW_GATE_LORA])
    log_w = -jax.nn.softplus(-_f32(w0 + jnp.tanh(wd) @ w2)) - 0.5
    heads = lambda t: _f32(t).reshape(B, T, RW_HEADS, RW_HD)
    decay = heads(jnp.exp(-jnp.exp(log_w)))
    a = heads(jax.nn.sigmoid(_f32(a0 + ad @ a2)))
    g = _f32(jax.nn.sigmoid(gd) @ g2)
    r, k, v = heads(r), heads(k), heads(v)
    kk = k * k_k.reshape(RW_HEADS, RW_HD)
    kk = kk * lax.rsqrt(jnp.maximum(jnp.sum(kk * kk, -1, keepdims=True), 1e-24))
    k = k * (1.0 + (a - 1.0) * k_a.reshape(RW_HEADS, RW_HD))

    def step(S, inp):
        r_t, w_t, k_t, v_t, kk_t, a_t = inp
        sa = jnp.einsum('bhvk,bhk->bhv', S, -kk_t)
        S = (S * w_t[:, :, None, :] + sa[..., None] * (kk_t * a_t)[:, :, None, :]
             + v_t[..., None] * k_t[:, :, None, :])
        return S, jnp.einsum('bhvk,bhk->bhv', S, r_t)

    time_major = lambda t: jnp.moveaxis(t, 1, 0)
    S_T, y = lax.scan(step, _f32(S0), tuple(time_major(t) for t in (r, decay, k, v, kk, a)))
    y = jnp.moveaxis(y, 0, 1)
    y = _standardize(y, RW_GN_EPS) * gn_g.reshape(RW_HEADS, RW_HD) + gn_b.reshape(RW_HEADS, RW_HD)
    bonus = jnp.sum(r * k * r_k, -1, keepdims=True) * v
    out = (y + bonus).reshape(B, T, RW_W) * g
    return out.astype(p.dtype), p[:, -1], S_T


def _gla_mix(p, S0, a2, a_b, gn_g):
    B, T, _ = p.shape
    q, k, v, gdown, gr = _split(p, [GLA_QK, GLA_QK, GLA_W, GLA_GATE_LORA, GLA_W])
    log_a = jax.nn.log_sigmoid(_f32(gdown @ a2 + a_b)) / GLA_TAU
    qk_heads = lambda t: _f32(t).reshape(B, T, GLA_HEADS, GLA_DK)
    q = qk_heads(q) * GLA_DK ** -0.5
    k, log_a = qk_heads(k), qk_heads(log_a)
    v = _f32(v).reshape(B, T, GLA_HEADS, GLA_DV)
    L_ch = math.gcd(T, GLA_CHUNK)
    n_ch = T // L_ch
    causal = jnp.tril(jnp.ones((L_ch, L_ch), bool))
    chunks = lambda t: jnp.moveaxis(t.reshape(B, n_ch, L_ch, *t.shape[2:]), 1, 0)

    def body(S, inp):
        qc, kc, vc, lac = inp
        b = jnp.cumsum(lac, axis=1)
        q_dec = qc * jnp.exp(b)
        A = jnp.einsum('bihd,bjhd->bhij', q_dec, kc * jnp.exp(-b))
        A = jnp.where(causal, A, 0.0)
        o = jnp.einsum('bhij,bjhv->bihv', A, vc) + jnp.einsum('bihd,bhdv->bihv', q_dec, S)
        b_last = b[:, -1]
        S = S * jnp.exp(b_last)[..., None] + jnp.einsum(
            'bjhd,bjhv->bhdv', kc * jnp.exp(b_last[:, None] - b), vc)
        return S, o

    S_T, o = lax.scan(body, _f32(S0), tuple(chunks(t) for t in (q, k, v, log_a)))
    o = jnp.moveaxis(o, 0, 1).reshape(B, T, GLA_HEADS, GLA_DV)
    o = o * lax.rsqrt(jnp.mean(o * o, -1, keepdims=True) + NORM_EPS) * gn_g.reshape(GLA_HEADS, GLA_DV)
    out = o.reshape(B, T, GLA_W) * jax.nn.silu(_f32(gr))
    return out.astype(p.dtype), S_T


def _mlstm_mix(p, conv0, C0, n0, m0, conv_w, conv_b, i_b, f_b, gn_g):
    B, T, _ = p.shape
    qk_raw, v, o_raw, i_raw, f_raw = _split(p, [2 * ML_W, ML_W, ML_W, ML_HEADS, ML_HEADS])
    xpad = jnp.concatenate([conv0.astype(p.dtype), qk_raw], axis=1)
    conv = conv_b
    for j in range(ML_CONV):
        conv = conv + xpad[:, j:j + T] * conv_w[j]
    q, k = _split(jax.nn.silu(_f32(conv)), [ML_W, ML_W])
    heads = lambda t: _f32(t).reshape(B, T, ML_HEADS, ML_HD)
    q, k, v = heads(q), heads(k) * ML_HD ** -0.5, heads(v)
    i_pre = _f32(i_raw + i_b)
    log_f = jax.nn.log_sigmoid(_f32(f_raw + f_b))
    L_ch = math.gcd(T, ML_CHUNK)
    n_ch = T // L_ch
    causal = jnp.tril(jnp.ones((L_ch, L_ch), bool))
    chunks = lambda t: jnp.moveaxis(t.reshape(B, n_ch, L_ch, *t.shape[2:]), 1, 0)

    def body(carry, inp):
        Cs, ns, ms = carry
        qc, kc, vc, ic, fc = inp
        b = jnp.moveaxis(jnp.cumsum(fc, axis=1), 1, 2)
        ig = jnp.moveaxis(ic, 1, 2)
        D = jnp.where(causal, b[..., :, None] - b[..., None, :] + ig[..., None, :], -jnp.inf)
        inter = b + ms[..., None]
        m_t = jnp.maximum(inter, jnp.max(D, -1))
        Dw = jnp.exp(D - m_t[..., None])
        w_inter = jnp.exp(inter - m_t)
        s = jnp.einsum('bihd,bjhd->bhij', qc, kc) * Dw
        num = jnp.einsum('bhij,bjhv->bhiv', s, vc) + w_inter[..., None] * jnp.einsum('bhvd,bihd->bhiv', Cs, qc)
        den = jnp.sum(s, -1) + w_inter * jnp.einsum('bhd,bihd->bhi', ns, qc)
        hc = num / jnp.maximum(jnp.abs(den), jnp.exp(-m_t))[..., None]
        m_last = m_t[..., -1]
        w_state = jnp.exp(b[..., -1] + ms - m_last)
        w_j = jnp.exp(b[..., -1:] - b + ig - m_last[..., None])
        Cs = Cs * w_state[..., None, None] + jnp.einsum('bhj,bjhv,bjhd->bhvd', w_j, vc, kc)
        ns = ns * w_state[..., None] + jnp.einsum('bhj,bjhd->bhd', w_j, kc)
        return (Cs, ns, m_last), jnp.moveaxis(hc, 1, 2)

    (C_T, n_T, m_T), h = lax.scan(body, (_f32(C0), _f32(n0), _f32(m0)),
                                  tuple(chunks(t) for t in (q, k, v, i_pre, log_f)))
    h = jnp.moveaxis(h, 0, 1).reshape(B, T, ML_HEADS, ML_HD)
    h = _standardize(h, NORM_EPS) * gn_g.reshape(ML_HEADS, ML_HD)
    out = h.reshape(B, T, ML_W) * jax.nn.sigmoid(_f32(o_raw))
    return out.astype(p.dtype), xpad[:, T:], C_T, n_T, m_T


def _layer(x, c, st, prm):
    (ada_w, ada_b, ln_g, ln_b, ffn_wg, ffn_wu, ffn_wd, w_in, w_out,
     rw_mu, rw_w0, rw_w2, rw_a0, rw_a2, rw_g2, rw_k_k, rw_k_a, rw_r_k, rw_gn_g, rw_gn_b,
     gla_a2, gla_a_b, gla_gn_g, ml_conv_w, ml_conv_b, ml_i_b, ml_f_b, ml_gn_g) = prm
    st_rw, st_shift, st_gla, st_c, st_n, st_m, st_conv = st
    B = x.shape[0]
    mods = (jax.nn.silu(c) @ ada_w + ada_b).reshape(B, 3, 3, 1, D_MODEL)

    def modulate(t, i):
        return t * (1.0 + mods[:, i, 1]) + mods[:, i, 0]

    def residual(t, i, out, w):
        return _layer_norm(ALPHA * t + w * (1.0 + mods[:, i, 2]) * out, ln_g[i], ln_b[i])

    x = residual(x, 0, _swiglu(modulate(x, 0), ffn_wg[0], ffn_wu[0], ffn_wd[0]), 0.5)
    proj = modulate(x, 1) @ w_in
    p_rw, p_gla, p_ml = _split(proj, [RW_PROJ, GLA_PROJ, ML_PROJ])
    o_rw, new_shift, new_rw = _rwkv7_mix(p_rw, st_shift, st_rw, rw_mu, rw_w0, rw_w2, rw_a0, rw_a2, rw_g2,
                                         rw_k_k, rw_k_a, rw_r_k, rw_gn_g, rw_gn_b)
    o_gla, new_gla = _gla_mix(p_gla, st_gla, gla_a2, gla_a_b, gla_gn_g)
    o_ml, new_conv, new_c, new_n, new_m = _mlstm_mix(p_ml, st_conv, st_c, st_n, st_m, ml_conv_w, ml_conv_b,
                                                     ml_i_b, ml_f_b, ml_gn_g)
    mix = jnp.concatenate([o_rw, o_gla, o_ml], axis=-1) @ w_out
    x = residual(x, 1, mix, 1.0)
    x = residual(x, 2, _swiglu(modulate(x, 2), ffn_wg[1], ffn_wu[1], ffn_wd[1]), 0.5)
    return x, (new_rw, new_shift, new_gla, new_c, new_n, new_m, new_conv)


def _run_trunk(x, c, states, params):
    collected = [[] for _ in states]
    for l in range(DEPTH):
        x, new = _layer(x, c, tuple(s[l] for s in states), tuple(w[l] for w in params))
        for lst, s in zip(collected, new):
            lst.append(s)
    return x, tuple(jnp.stack(lst) for lst in collected)


def setup_inputs(seed: int = 0) -> dict:
    key = jax.random.key(seed)
    ks = iter(jax.random.split(key, 64))
    nrm = lambda shape, scale: scale * jax.random.normal(next(ks), shape, jnp.float32)
    uni = lambda shape, lo, hi: jax.random.uniform(next(ks), shape, jnp.float32, lo, hi)
    L = DEPTH
    return {
        'x_prompt': nrm((BATCH, SEQ, D_MODEL), 1.0),
        'x_sample': nrm((DEC_BATCH, DEC_SEQ, D_MODEL), 1.0),
        'c_prompt': nrm((BATCH, D_MODEL), 1.0),
        'c_sample': nrm((DEC_BATCH, D_MODEL), 1.0),
        'state_rwkv': nrm((L, DEC_BATCH, RW_HEADS, RW_HD, RW_HD), 0.1),
        'state_rwkv_shift': nrm((L, DEC_BATCH, RW_PROJ), 1.0),
        'state_gla': nrm((L, DEC_BATCH, GLA_HEADS, GLA_DK, GLA_DV), 0.1),
        'state_mlstm_c': nrm((L, DEC_BATCH, ML_HEADS, ML_HD, ML_HD), 0.1),
        'state_mlstm_n': nrm((L, DEC_BATCH, ML_HEADS, ML_HD), 0.5),
        'state_mlstm_m': nrm((L, DEC_BATCH, ML_HEADS), 1.0),
        'state_mlstm_conv': nrm((L, DEC_BATCH, ML_CONV - 1, 2 * ML_W), 1.0),
        'ada_w': nrm((L, D_MODEL, 9 * D_MODEL), 0.2 * D_MODEL ** -0.5),
        'ada_b': nrm((L, 9 * D_MODEL), 0.02),
        'ln_g': 1.0 + nrm((L, 3, D_MODEL), 0.02),
        'ln_b': nrm((L, 3, D_MODEL), 0.02),
        'ffn_wg': nrm((L, 2, D_MODEL, D_FF), D_MODEL ** -0.5),
        'ffn_wu': nrm((L, 2, D_MODEL, D_FF), D_MODEL ** -0.5),
        'ffn_wd': nrm((L, 2, D_FF, D_MODEL), BETA * D_FF ** -0.5),
        'w_in': nrm((L, D_MODEL, D_IN), D_MODEL ** -0.5),
        'w_out': nrm((L, D_MIX, D_MODEL), BETA * D_MIX ** -0.5),
        'rw_mu': uni((L, RW_PROJ), 0.0, 1.0),
        'rw_w0': uni((L, RW_W), -5.0, 1.0),
        'rw_w2': nrm((L, RW_DECAY_LORA, RW_W), 0.5 * RW_DECAY_LORA ** -0.5),
        'rw_a0': nrm((L, RW_W), 0.1),
        'rw_a2': nrm((L, RW_AAA_LORA, RW_W), RW_AAA_LORA ** -0.5),
        'rw_g2': nrm((L, RW_GATE_LORA, RW_W), RW_GATE_LORA ** -0.5),
        'rw_k_k': 0.85 + nrm((L, RW_W), 0.05),
        'rw_k_a': 1.0 + nrm((L, RW_W), 0.05),
        'rw_r_k': nrm((L, RW_HEADS, RW_HD), 0.1),
        'rw_gn_g': 1.0 + nrm((L, RW_W), 0.02),
        'rw_gn_b': nrm((L, RW_W), 0.02),
        'gla_a2': nrm((L, GLA_GATE_LORA, GLA_QK), GLA_GATE_LORA ** -0.5),
        'gla_a_b': 1.0 + nrm((L, GLA_QK), 0.5),
        'gla_gn_g': 1.0 + nrm((L, GLA_W), 0.02),
        'ml_conv_w': nrm((L, ML_CONV, 2 * ML_W), 0.5),
        'ml_conv_b': nrm((L, 2 * ML_W), 0.02),
        'ml_i_b': nrm((L, ML_HEADS), 0.1),
        'ml_f_b': uni((L, ML_HEADS), 3.0, 6.0),
        'ml_gn_g': 1.0 + nrm((L, ML_W), 0.02),
    }


def reference(x_prompt, x_sample, c_prompt, c_sample, state_rwkv, state_rwkv_shift, state_gla,
              state_mlstm_c, state_mlstm_n, state_mlstm_m, state_mlstm_conv,
              ada_w, ada_b, ln_g, ln_b, ffn_wg, ffn_wu, ffn_wd, w_in, w_out,
              rw_mu, rw_w0, rw_w2, rw_a0, rw_a2, rw_g2, rw_k_k, rw_k_a, rw_r_k, rw_gn_g, rw_gn_b,
              gla_a2, gla_a_b, gla_gn_g, ml_conv_w, ml_conv_b, ml_i_b, ml_f_b, ml_gn_g):
    params = (ada_w, ada_b, ln_g, ln_b, ffn_wg, ffn_wu, ffn_wd, w_in, w_out,
              rw_mu, rw_w0, rw_w2, rw_a0, rw_a2, rw_g2, rw_k_k, rw_k_a, rw_r_k, rw_gn_g, rw_gn_b,
              gla_a2, gla_a_b, gla_gn_g, ml_conv_w, ml_conv_b, ml_i_b, ml_f_b, ml_gn_g)
    Bp = x_prompt.shape[0]
    dt = x_prompt.dtype
    zero_states = (
        jnp.zeros((DEPTH, Bp, RW_HEADS, RW_HD, RW_HD), dt),
        jnp.zeros((DEPTH, Bp, RW_PROJ), dt),
        jnp.zeros((DEPTH, Bp, GLA_HEADS, GLA_DK, GLA_DV), dt),
        jnp.zeros((DEPTH, Bp, ML_HEADS, ML_HD, ML_HD), dt),
        jnp.zeros((DEPTH, Bp, ML_HEADS, ML_HD), dt),
        jnp.zeros((DEPTH, Bp, ML_HEADS), dt),
        jnp.zeros((DEPTH, Bp, ML_CONV - 1, 2 * ML_W), dt),
    )
    y_prompt, p_states = _run_trunk(x_prompt, c_prompt, zero_states, params)
    p_rwkv, p_rwkv_shift, p_gla, p_mlstm_c, p_mlstm_n, p_mlstm_m, p_mlstm_conv = p_states
    sample_states = (state_rwkv, state_rwkv_shift, state_gla, state_mlstm_c, state_mlstm_n,
                     state_mlstm_m, state_mlstm_conv)
    y_sample, s_states = _run_trunk(x_sample, c_sample, sample_states, params)
    s_rwkv, s_rwkv_shift, s_gla, s_mlstm_c, s_mlstm_n, s_mlstm_m, s_mlstm_conv = s_states
    return (y_prompt, y_sample, p_rwkv, p_rwkv_shift, p_gla, p_mlstm_c, p_mlstm_n, p_mlstm_m, p_mlstm_conv,
            s_rwkv, s_rwkv_shift, s_gla, s_mlstm_c, s_mlstm_n, s_mlstm_m, s_mlstm_conv)
```

```python
import functools
import math

import jax
import jax.numpy as jnp
from jax import lax
from jax.experimental import pallas as pl
from jax.experimental.pallas import tpu as pltpu

F32 = jnp.float32
BF16 = jnp.bfloat16

D_MODEL = 1024
DEPTH = 4
D_FF = 2752
D_FF_PAD = 2816
FF_CHUNKS = 2

RW_HEADS, RW_HD = 6, 64
RW_W = RW_HEADS * RW_HD
RW_PROJ = 3 * RW_W + 64 + 64 + 128
RW_GN_EPS = 64e-5

GLA_HEADS, GLA_DK, GLA_DV = 4, 48, 96
GLA_DK_PAD, GLA_DV_PAD = 64, 128
GLA_QK_PAD = GLA_HEADS * GLA_DK_PAD
GLA_W_PAD = GLA_HEADS * GLA_DV_PAD
GLA_LORA = 16
GLA_PROJ_PAD = 2 * GLA_QK_PAD + 2 * GLA_W_PAD + 128
GLA_TAU = 16.0

ML_HEADS, ML_HD = 4, 64
ML_W = ML_HEADS * ML_HD
ML_CONV = 4
ML_PROJ_PAD = 4 * ML_W + 2 * 128

MIX_CHUNK = 64
ALPHA = (2 * DEPTH) ** 0.25
LN_EPS = 1e-5
NORM_EPS = 1e-6

VMEM_LIMIT = 56 * 1024 * 1024

_NN = (((1,), (0,)), ((), ()))
_NT = (((1,), (1,)), ((), ()))
_TN = (((0,), (0,)), ((), ()))


def _dg(a, b, dims=_NN):
    return lax.dot_general(a, b, dims, preferred_element_type=F32)


def _split2(x):
    hi = x.astype(BF16)
    lo = (x - hi.astype(F32)).astype(BF16)
    return hi, lo


def _split3(x):
    x1 = x.astype(BF16)
    r1 = x - x1.astype(F32)
    x2 = r1.astype(BF16)
    x3 = (r1 - x2.astype(F32)).astype(BF16)
    return x1, x2, x3


def _dot3(a, b, dims=_NN):
    ah, al = _split2(a)
    bh, bl = _split2(b)
    return _dg(ah, bh, dims) + (_dg(ah, bl, dims) + _dg(al, bh, dims))


def _dot_mask_lhs(m_bf16, x, dims=_NN):
    x1, x2, x3 = _split3(x)
    return _dg(m_bf16, x1, dims) + (_dg(m_bf16, x2, dims) + _dg(m_bf16, x3, dims))


def _dot_mask_rhs(x, m_bf16):
    xh, xl = _split2(x)
    return _dg(xh, m_bf16) + _dg(xl, m_bf16)


def _sigmoid(x):
    return 1.0 / (1.0 + jnp.exp(-x))


def _log_sigmoid(x):
    return jnp.minimum(x, 0.0) - jnp.log1p(jnp.exp(-jnp.abs(x)))


def _tri_masks(n):
    row = lax.broadcasted_iota(jnp.int32, (n, n), 0)
    col = lax.broadcasted_iota(jnp.int32, (n, n), 1)
    return row >= col, row > col, row == col


def _modulate(x3, sc_ref, sh_ref):
    return x3 * (1.0 + sc_ref[:, 0]) + sh_ref[:, 0]


def _residual_ln(x3, upd3, gt_ref, lg_ref, lb_ref, w):
    y = ALPHA * x3 + (w * (1.0 + gt_ref[:, 0])) * upd3
    mu = jnp.mean(y, -1, keepdims=True)
    yc = y - mu
    var = jnp.mean(yc * yc, -1, keepdims=True)
    return yc * lax.rsqrt(var + LN_EPS) * lg_ref[...] + lb_ref[...]


def _ada_kernel(c_ref, w_ref, b_ref, o_ref):
    c = c_ref[...]
    s = (c * _sigmoid(c)).astype(BF16)
    o_ref[0] = _dg(s, w_ref[0].astype(BF16)) + b_ref[0]


def _ada_call(c_all, ada_w, ada_b):
    nb = c_all.shape[0]
    tn = 1536
    return pl.pallas_call(
        _ada_kernel,
        grid=(DEPTH, 9 * D_MODEL // tn),
        in_specs=[
            pl.BlockSpec((nb, D_MODEL), lambda l, j: (0, 0)),
            pl.BlockSpec((1, D_MODEL, tn), lambda l, j: (l, 0, j)),
            pl.BlockSpec((1, 1, tn), lambda l, j: (l, 0, j)),
        ],
        out_specs=pl.BlockSpec((1, nb, tn), lambda l, j: (l, 0, j)),
        out_shape=jax.ShapeDtypeStruct((DEPTH, nb, 9 * D_MODEL), F32),
        compiler_params=pltpu.CompilerParams(
            dimension_semantics=("parallel", "parallel"), vmem_limit_bytes=VMEM_LIMIT),
        name="ada_mods",
    )(c_all, ada_w, ada_b.reshape(DEPTH, 1, 9 * D_MODEL))


def _mod_spec(bb, k):
    return pl.BlockSpec((bb, 1, 1, D_MODEL), lambda i, j, k=k: (i, k, 0, 0))


def _const_spec(shape):
    nd = len(shape)
    return pl.BlockSpec(shape, lambda i, j, nd=nd: (0,) * nd, pipeline_mode=pl.Buffered(1))


def _ffn_kernel(x_ref, sh_ref, sc_ref, gt_ref, wg_ref, wu_ref, wd_ref, lg_ref, lb_ref, o_ref):
    bb, tt, d = x_ref.shape
    x3 = x_ref[...]
    h = _modulate(x3, sc_ref, sh_ref).reshape(bb * tt, d).astype(BF16)
    ck = D_FF_PAD // FF_CHUNKS
    acc = jnp.zeros((bb * tt, d), F32)
    for c in range(FF_CHUNKS):
        g = _dg(h, wg_ref[:, c * ck:(c + 1) * ck])
        u = _dg(h, wu_ref[:, c * ck:(c + 1) * ck])
        a = (g * _sigmoid(g) * u).astype(BF16)
        acc = acc + _dg(a, wd_ref[c * ck:(c + 1) * ck, :])
    o_ref[...] = _residual_ln(x3, acc.reshape(bb, tt, d), gt_ref, lg_ref, lb_ref, 0.5)


def _ffn_call(x, mods, sub, wg, wu, wd, lg, lb, bb, tt):
    B, T, d = x.shape
    return pl.pallas_call(
        _ffn_kernel,
        grid=(B // bb, T // tt),
        in_specs=[
            pl.BlockSpec((bb, tt, d), lambda i, j: (i, j, 0)),
            _mod_spec(bb, 3 * sub), _mod_spec(bb, 3 * sub + 1), _mod_spec(bb, 3 * sub + 2),
            _const_spec((d, D_FF_PAD)), _const_spec((d, D_FF_PAD)), _const_spec((D_FF_PAD, d)),
            _const_spec((1, d)), _const_spec((1, d)),
        ],
        out_specs=pl.BlockSpec((bb, tt, d), lambda i, j: (i, j, 0)),
        out_shape=jax.ShapeDtypeStruct((B, T, d), F32),
        compiler_params=pltpu.CompilerParams(
            dimension_semantics=("parallel", "parallel"), vmem_limit_bytes=VMEM_LIMIT),
        name="ffn_sublayer",
    )(x, mods, mods, mods, wg, wu, wd, lg, lb)


def _inproj_kernel(x_ref, sh_ref, sc_ref, w_ref, orw_ref, ogla_ref, oml_ref):
    bb, tt, d = x_ref.shape
    h = _modulate(x_ref[...], sc_ref, sh_ref).reshape(bb * tt, d).astype(BF16)
    o1, o2 = RW_PROJ, RW_PROJ + GLA_PROJ_PAD
    orw_ref[...] = _dg(h, w_ref[:, :o1]).reshape(bb, tt, RW_PROJ)
    ogla_ref[...] = _dg(h, w_ref[:, o1:o2]).reshape(bb, tt, GLA_PROJ_PAD)
    oml_ref[...] = _dg(h, w_ref[:, o2:]).reshape(bb, tt, ML_PROJ_PAD)


def _inproj_call(x, mods, w_in, bb, tt):
    B, T, d = x.shape
    widths = (RW_PROJ, GLA_PROJ_PAD, ML_PROJ_PAD)
    return pl.pallas_call(
        _inproj_kernel,
        grid=(B // bb, T // tt),
        in_specs=[
            pl.BlockSpec((bb, tt, d), lambda i, j: (i, j, 0)),
            _mod_spec(bb, 3), _mod_spec(bb, 4),
            _const_spec((d, sum(widths))),
        ],
        out_specs=[pl.BlockSpec((bb, tt, w), lambda i, j: (i, j, 0)) for w in widths],
        out_shape=[jax.ShapeDtypeStruct((B, T, w), F32) for w in widths],
        compiler_params=pltpu.CompilerParams(
            dimension_semantics=("parallel", "parallel"), vmem_limit_bytes=VMEM_LIMIT),
        name="mixer_in_proj",
    )(x, mods, mods, w_in)


def _outproj_kernel(x_ref, gt_ref, orw_ref, ogla_ref, oml_ref, w1_ref, w2_ref, w3_ref,
                    lg_ref, lb_ref, o_ref):
    bb, tt, d = x_ref.shape
    flat = lambda ref: ref[...].reshape(bb * tt, ref.shape[-1]).astype(BF16)
    mix = _dg(flat(orw_ref), w1_ref[...]) + _dg(flat(ogla_ref), w2_ref[...]) + _dg(flat(oml_ref), w3_ref[...])
    o_ref[...] = _residual_ln(x_ref[...], mix.reshape(bb, tt, d), gt_ref, lg_ref, lb_ref, 1.0)


def _outproj_call(x, mods, o_rw, o_gla, o_ml, w1, w2, w3, lg, lb, bb, tt):
    B, T, d = x.shape
    tok = lambda w: pl.BlockSpec((bb, tt, w), lambda i, j: (i, j, 0))
    return pl.pallas_call(
        _outproj_kernel,
        grid=(B // bb, T // tt),
        in_specs=[
            tok(d), _mod_spec(bb, 5), tok(RW_W), tok(GLA_W_PAD), tok(ML_W),
            _const_spec((RW_W, d)), _const_spec((GLA_W_PAD, d)), _const_spec((ML_W, d)),
            _const_spec((1, d)), _const_spec((1, d)),
        ],
        out_specs=tok(d),
        out_shape=jax.ShapeDtypeStruct((B, T, d), F32),
        compiler_params=pltpu.CompilerParams(
            dimension_semantics=("parallel", "parallel"), vmem_limit_bytes=VMEM_LIMIT),
        name="mixer_out_proj",
    )(x, mods, o_rw, o_gla, o_ml, w1, w2, w3, lg, lb)


def _cumsum_rows(x, tri_bf16):
    return _dot_mask_lhs(tri_bf16, x)


def _rwkv_kernel(p_ref, sh0_ref, s0_ref, mu_ref, w0_ref, a0_ref, wl_ref, g2_ref, kk_ref, ka_ref,
                 rk_ref, gg_ref, gb_ref, seg_ref, o_ref, sT_ref, s_scr, prev_scr):
    L = p_ref.shape[1]
    ci = pl.program_id(1)

    @pl.when(ci == 0)
    def _():
        s_scr[...] = s0_ref[0]
        prev_scr[...] = jnp.broadcast_to(sh0_ref[0], prev_scr.shape)

    p = p_ref[0]
    row = lax.broadcasted_iota(jnp.int32, (L, 1), 0)
    prev = jnp.where(row == 0, prev_scr[0:1, :], pltpu.roll(p, 1, 0))
    prev_scr[...] = jnp.broadcast_to(p[L - 1:L, :], prev_scr.shape)
    xs = p + (prev - p) * mu_ref[...]

    r = xs[:, 0:RW_W]
    k = xs[:, RW_W:2 * RW_W]
    v = xs[:, 2 * RW_W:3 * RW_W]
    lin = xs[:, 3 * RW_W:3 * RW_W + 128]
    lane = lax.broadcasted_iota(jnp.int32, (1, 128), 1)
    lin = jnp.where(lane < 64, jnp.tanh(lin), lin)
    dl = _dg(lin.astype(BF16), wl_ref[...])
    lw = -jnp.exp(_log_sigmoid(w0_ref[...] + dl[:, :RW_W]) - 0.5)
    a_sig = _sigmoid(a0_ref[...] + dl[:, RW_W:])
    g = _dg(_sigmoid(xs[:, 3 * RW_W + 128:]).astype(BF16), g2_ref[...])

    seg = seg_ref[...]
    kk = k * kk_ref[...]
    kk = kk * lax.rsqrt(jnp.maximum(_dot_mask_rhs(kk * kk, seg), 1e-24))
    k2 = k * (1.0 + (a_sig - 1.0) * ka_ref[...])
    bv = kk * a_sig

    incl, strict, eye = _tri_masks(L)
    c = _cumsum_rows(lw, incl.astype(BF16))
    cl = c[L - 1:L, :]
    einv = jnp.exp(-c)
    edl = jnp.exp(cl - c)
    a_t = -kk * jnp.exp(c - lw)
    r_t = r * jnp.exp(c)
    k_h = k2 * einv
    b_h = bv * einv
    k_e = k2 * edl
    b_e = bv * edl
    p_end = jnp.exp(cl)

    ys = []
    for h in range(RW_HEADS):
        sl = slice(h * RW_HD, (h + 1) * RW_HD)
        ar = jnp.concatenate([a_t[:, sl], r_t[:, sl]], axis=0)
        gk = _dot3(ar, k_h[:, sl], _NT)
        gb = _dot3(ar, b_h[:, sl], _NT)
        a_ak = jnp.where(strict, gk[:L], 0.0)
        a_rk = jnp.where(incl, gk[L:], 0.0)
        a_ab = jnp.where(strict, gb[:L], 0.0)
        a_rb = jnp.where(incl, gb[L:], 0.0)
        t_inv = jnp.where(eye, 1.0, a_ab)
        pw = a_ab
        for _ in range(int(math.log2(L)) - 1):
            pw = _dot3(pw, pw)
            t_inv = t_inv + _dot3(t_inv, pw)
        s0 = s_scr[h]
        vh = v[:, sl]
        u = _dot3(t_inv, _dot3(a_t[:, sl], s0, _NT) + _dot3(a_ak, vh))
        ys.append(_dot3(r_t[:, sl], s0, _NT) + _dot3(a_rk, vh) + _dot3(a_rb, u))
        s_scr[h] = s0 * p_end[:, sl] + _dot3(vh, k_e[:, sl], _TN) + _dot3(u, b_e[:, sl], _TN)
    y = jnp.concatenate(ys, axis=-1)

    inv_hd = 1.0 / RW_HD
    yc = y - _dot_mask_rhs(y, seg) * inv_hd
    var = _dot_mask_rhs(yc * yc, seg) * inv_hd
    yn = yc * lax.rsqrt(var + RW_GN_EPS) * gg_ref[...] + gb_ref[...]
    bonus = _dot_mask_rhs(r * k2 * rk_ref[...], seg) * v
    o_ref[0] = (yn + bonus) * g

    @pl.when(ci == pl.num_programs(1) - 1)
    def _():
        sT_ref[0] = s_scr[...]


def _seq_spec(shape):
    nd = len(shape)
    return pl.BlockSpec((1,) + shape, lambda b, c, nd=nd: (b,) + (0,) * nd)


def _param_spec(shape):
    nd = len(shape)
    return pl.BlockSpec(shape, lambda b, c, nd=nd: (0,) * nd)


def _mixer_params():
    return pltpu.CompilerParams(dimension_semantics=("parallel", "arbitrary"), vmem_limit_bytes=VMEM_LIMIT)


def _rwkv_call(p, shift0, s0, prm, L):
    B, T, _ = p.shape
    mu, w0, a0, wl, g2, k_k, k_a, r_k, gn_g, gn_b, seg = prm
    row = lambda w: _param_spec((1, w))
    return pl.pallas_call(
        _rwkv_kernel,
        grid=(B, T // L),
        in_specs=[
            pl.BlockSpec((1, L, RW_PROJ), lambda b, c: (b, c, 0)),
            _seq_spec((1, RW_PROJ)), _seq_spec((RW_HEADS, RW_HD, RW_HD)),
            row(RW_PROJ), row(RW_W), row(RW_W), _param_spec((128, 2 * RW_W)), _param_spec((128, RW_W)),
            row(RW_W), row(RW_W), row(RW_W), row(RW_W), row(RW_W), _param_spec((RW_W, RW_W)),
        ],
        out_specs=[pl.BlockSpec((1, L, RW_W), lambda b, c: (b, c, 0)),
                   _seq_spec((RW_HEADS, RW_HD, RW_HD))],
        out_shape=[jax.ShapeDtypeStruct((B, T, RW_W), F32),
                   jax.ShapeDtypeStruct((B, RW_HEADS, RW_HD, RW_HD), F32)],
        scratch_shapes=[pltpu.VMEM((RW_HEADS, RW_HD, RW_HD), F32), pltpu.VMEM((8, RW_PROJ), F32)],
        compiler_params=_mixer_params(),
        name="rwkv7_mixer",
    )(p, shift0.reshape(B, 1, RW_PROJ), s0, mu, w0, a0, wl, g2, k_k, k_a, r_k, gn_g, gn_b, seg)


def _gla_kernel(p_ref, s0_ref, a2_ref, ab_ref, gg_ref, o_ref, sT_ref, s_scr):
    L = p_ref.shape[1]
    ci = pl.program_id(1)

    @pl.when(ci == 0)
    def _():
        s_scr[...] = s0_ref[0]

    p = p_ref[0]
    q = p[:, 0:GLA_QK_PAD]
    k = p[:, GLA_QK_PAD:2 * GLA_QK_PAD]
    v = p[:, 2 * GLA_QK_PAD:2 * GLA_QK_PAD + GLA_W_PAD]
    gr = p[:, 2 * GLA_QK_PAD + GLA_W_PAD:2 * GLA_QK_PAD + 2 * GLA_W_PAD]
    gd = p[:, 2 * GLA_QK_PAD + 2 * GLA_W_PAD:]
    log_a = _log_sigmoid(_dg(gd.astype(BF16), a2_ref[...]) + ab_ref[...]) * (1.0 / GLA_TAU)

    incl, _, eye = _tri_masks(L)
    bc = _cumsum_rows(log_a, incl.astype(BF16))
    bl = bc[L - 1:L, :]
    qd = q * (GLA_DK ** -0.5) * jnp.exp(bc)
    kd = k * jnp.exp(-bc)
    ks = k * jnp.exp(bl - bc)
    d_end = jnp.exp(bl)
    _, _, eye_k = _tri_masks(GLA_DK_PAD)

    outs = []
    for h in range(GLA_HEADS):
        sk = slice(h * GLA_DK_PAD, (h + 1) * GLA_DK_PAD)
        sv = slice(h * GLA_DV_PAD, (h + 1) * GLA_DV_PAD)
        vh = v[:, sv]
        a = jnp.where(incl, _dot3(qd[:, sk], kd[:, sk], _NT), 0.0)
        s0 = s_scr[h]
        o = _dot3(a, vh) + _dot3(qd[:, sk], s0)
        dmat = jnp.where(eye_k, jnp.broadcast_to(d_end[:, sk], (GLA_DK_PAD, GLA_DK_PAD)), 0.0)
        s_scr[h] = _dot3(dmat, s0) + _dot3(ks[:, sk], vh, _TN)
        ms = jnp.sum(o * o, -1, keepdims=True) * (1.0 / GLA_DV)
        grh = gr[:, sv]
        outs.append(o * lax.rsqrt(ms + NORM_EPS) * gg_ref[:, sv] * (grh * _sigmoid(grh)))
    o_ref[0] = jnp.concatenate(outs, axis=-1)

    @pl.when(ci == pl.num_programs(1) - 1)
    def _():
        sT_ref[0] = s_scr[...]


def _gla_call(p, s0, prm, L):
    B, T, _ = p.shape
    a2, a_b, gn_g = prm
    st = (GLA_HEADS, GLA_DK_PAD, GLA_DV_PAD)
    return pl.pallas_call(
        _gla_kernel,
        grid=(B, T // L),
        in_specs=[
            pl.BlockSpec((1, L, GLA_PROJ_PAD), lambda b, c: (b, c, 0)),
            _seq_spec(st),
            _param_spec((128, GLA_QK_PAD)), _param_spec((1, GLA_QK_PAD)), _param_spec((1, GLA_W_PAD)),
        ],
        out_specs=[pl.BlockSpec((1, L, GLA_W_PAD), lambda b, c: (b, c, 0)), _seq_spec(st)],
        out_shape=[jax.ShapeDtypeStruct((B, T, GLA_W_PAD), F32), jax.ShapeDtypeStruct((B,) + st, F32)],
        scratch_shapes=[pltpu.VMEM(st, F32)],
        compiler_params=_mixer_params(),
        name="gla_mixer",
    )(p, s0, a2, a_b, gn_g)


def _mlstm_kernel(p_ref, cv0_ref, c0_ref, n0_ref, m0_ref, cw_ref, cb_ref, ib_ref, fb_ref, gg_ref,
                  o_ref, cT_ref, nT_ref, mT_ref, c_scr, n_scr, m_scr, cv_scr):
    L = p_ref.shape[1]
    ci = pl.program_id(1)

    @pl.when(ci == 0)
    def _():
        c_scr[...] = c0_ref[0]
        n_scr[...] = n0_ref[0]
        m_scr[...] = m0_ref[0]
        cv_scr[...] = cv0_ref[0]

    p = p_ref[0]
    x = p[:, 0:2 * ML_W]
    row = lax.broadcasted_iota(jnp.int32, (L, 1), 0)
    carry = cv_scr[...]
    conv = cb_ref[...] + x * cw_ref[ML_CONV - 1:ML_CONV, :]
    for s in range(1, ML_CONV):
        cr = pltpu.roll(carry, s, 0)
        if L > 8:
            cr = jnp.concatenate([cr] * (L // 8), axis=0)
        shifted = jnp.where(row < s, cr, pltpu.roll(x, s, 0))
        conv = conv + shifted * cw_ref[ML_CONV - 1 - s:ML_CONV - s, :]
    cv_scr[...] = x[L - 8:L, :]
    qk = conv * _sigmoid(conv)
    q = qk[:, :ML_W]
    k = qk[:, ML_W:] * (ML_HD ** -0.5)
    v = p[:, 2 * ML_W:3 * ML_W]
    o_raw = p[:, 3 * ML_W:4 * ML_W]
    gi = p[:, 4 * ML_W:4 * ML_W + 128] + ib_ref[...]
    lf = _log_sigmoid(p[:, 4 * ML_W + 128:] + fb_ref[...])

    incl, _, _ = _tri_masks(L)
    b = _cumsum_rows(lf, incl.astype(BF16))
    bl = b[L - 1:L, :]
    m_prev = m_scr[...]
    inter = b + m_prev
    d = gi - b
    lane = lax.broadcasted_iota(jnp.int32, (1, 128), 1)
    ones_l = jnp.ones((L, 128), BF16)
    m_new = m_prev

    outs = []
    for h in range(ML_HEADS):
        sl = slice(h * ML_HD, (h + 1) * ML_HD)
        hot = lane == h
        pick = lambda t: jnp.sum(jnp.where(hot, t, 0.0), -1, keepdims=True)
        b_col, gi_col, inter_col = pick(b), pick(gi), pick(inter)
        bl_h, mp_h = pick(bl), pick(m_prev)
        d_row = _dot_mask_lhs(ones_l, jnp.where(hot, d, 0.0), _NT)
        dmat = jnp.where(incl, b_col + d_row, -jnp.inf)
        m_t = jnp.maximum(inter_col, jnp.max(dmat, -1, keepdims=True))
        dw = jnp.exp(dmat - m_t)
        w_int = jnp.exp(inter_col - m_t)
        qh, kh, vh = q[:, sl], k[:, sl], v[:, sl]
        s = _dot3(qh, kh, _NT) * dw
        cmat = c_scr[h]
        n_row = n_scr[h:h + 1, :]
        num = _dot3(s, vh) + w_int * _dot3(qh, cmat, _NT)
        den = jnp.sum(s, -1, keepdims=True) + w_int * jnp.sum(qh * n_row, -1, keepdims=True)
        hc = num / jnp.maximum(jnp.abs(den), jnp.exp(-m_t))
        m_last = m_t[L - 1:L, :]
        w_state = jnp.exp(bl_h + mp_h - m_last)
        w_j = jnp.exp(bl_h - b_col + gi_col - m_last)
        c_scr[h] = cmat * w_state + _dot3(w_j * vh, kh, _TN)
        n_scr[h:h + 1, :] = n_row * w_state + jnp.sum(w_j * kh, 0, keepdims=True)
        m_new = jnp.where(hot, m_last, m_new)
        mu = jnp.mean(hc, -1, keepdims=True)
        hcc = hc - mu
        var = jnp.mean(hcc * hcc, -1, keepdims=True)
        outs.append(hcc * lax.rsqrt(var + NORM_EPS) * gg_ref[:, sl])
    m_scr[...] = m_new
    o_ref[0] = jnp.concatenate(outs, axis=-1) * _sigmoid(o_raw)

    @pl.when(ci == pl.num_programs(1) - 1)
    def _():
        cT_ref[0] = c_scr[...]
        nT_ref[0] = n_scr[...]
        mT_ref[0] = m_scr[...]


def _mlstm_call(p, conv0, c0, n0, m0, prm, L):
    B, T, _ = p.shape
    cw, cb, ib, fb, gn_g = prm
    return pl.pallas_call(
        _mlstm_kernel,
        grid=(B, T // L),
        in_specs=[
            pl.BlockSpec((1, L, ML_PROJ_PAD), lambda b, c: (b, c, 0)),
            _seq_spec((8, 2 * ML_W)), _seq_spec((ML_HEADS, ML_HD, ML_HD)), _seq_spec((ML_HEADS, ML_HD)),
            _seq_spec((1, 128)),
            _param_spec((ML_CONV, 2 * ML_W)), _param_spec((1, 2 * ML_W)), _param_spec((1, 128)),
            _param_spec((1, 128)), _param_spec((1, ML_W)),
        ],
        out_specs=[pl.BlockSpec((1, L, ML_W), lambda b, c: (b, c, 0)),
                   _seq_spec((ML_HEADS, ML_HD, ML_HD)), _seq_spec((ML_HEADS, ML_HD)), _seq_spec((1, 128))],
        out_shape=[jax.ShapeDtypeStruct((B, T, ML_W), F32),
                   jax.ShapeDtypeStruct((B, ML_HEADS, ML_HD, ML_HD), F32),
                   jax.ShapeDtypeStruct((B, ML_HEADS, ML_HD), F32),
                   jax.ShapeDtypeStruct((B, 1, 128), F32)],
        scratch_shapes=[pltpu.VMEM((ML_HEADS, ML_HD, ML_HD), F32), pltpu.VMEM((ML_HEADS, ML_HD), F32),
                        pltpu.VMEM((1, 128), F32), pltpu.VMEM((8, 2 * ML_W), F32)],
        compiler_params=_mixer_params(),
        name="mlstm_mixer",
    )(p, conv0, c0, n0, m0, cw, cb, ib, fb, gn_g)


def _pad_heads(w, heads, width, padded):
    lead = w.shape[:-1]
    w = w.reshape(lead + (heads, width))
    w = jnp.pad(w, [(0, 0)] * len(lead) + [(0, 0), (0, padded - width)])
    return w.reshape(lead + (heads * padded,))


def _pad_last(w, n):
    return jnp.pad(w, [(0, 0)] * (w.ndim - 1) + [(0, n - w.shape[-1])])


def _prep_layer(l, W):
    gla_qk, gla_w = GLA_HEADS * GLA_DK, GLA_HEADS * GLA_DV
    w_in = W['w_in'][l]
    w_rw = w_in[:, :RW_PROJ]
    g = w_in[:, RW_PROJ:RW_PROJ + 2 * gla_qk + 2 * gla_w + GLA_LORA]
    m = w_in[:, RW_PROJ + 2 * gla_qk + 2 * gla_w + GLA_LORA:]
    g_q, g_k = g[:, :gla_qk], g[:, gla_qk:2 * gla_qk]
    g_v = g[:, 2 * gla_qk:2 * gla_qk + gla_w]
    g_gd = g[:, 2 * gla_qk + gla_w:2 * gla_qk + gla_w + GLA_LORA]
    g_gr = g[:, 2 * gla_qk + gla_w + GLA_LORA:]
    w_gla = jnp.concatenate([
        _pad_heads(g_q, GLA_HEADS, GLA_DK, GLA_DK_PAD), _pad_heads(g_k, GLA_HEADS, GLA_DK, GLA_DK_PAD),
        _pad_heads(g_v, GLA_HEADS, GLA_DV, GLA_DV_PAD), _pad_heads(g_gr, GLA_HEADS, GLA_DV, GLA_DV_PAD),
        _pad_last(g_gd, 128)], axis=1)
    w_ml = jnp.concatenate([m[:, :4 * ML_W], _pad_last(m[:, 4 * ML_W:4 * ML_W + ML_HEADS], 128),
                            _pad_last(m[:, 4 * ML_W + ML_HEADS:], 128)], axis=1)
    w_out = W['w_out'][l]
    w_o_gla = w_out[RW_W:RW_W + gla_w].reshape(GLA_HEADS, GLA_DV, D_MODEL)
    w_o_gla = jnp.pad(w_o_gla, ((0, 0), (0, GLA_DV_PAD - GLA_DV), (0, 0))).reshape(GLA_W_PAD, D_MODEL)

    zeros64 = jnp.zeros((64, RW_W), F32)
    w_lora = jnp.concatenate([jnp.concatenate([W['rw_w2'][l], zeros64], 1),
                              jnp.concatenate([zeros64, W['rw_a2'][l]], 1)], 0)
    a2 = jnp.pad(_pad_heads(W['gla_a2'][l], GLA_HEADS, GLA_DK, GLA_DK_PAD), ((0, 128 - GLA_LORA), (0, 0)))
    row = lambda t: t.reshape(1, -1)
    return dict(
        ffn_wg=[_pad_last(W['ffn_wg'][l, i], D_FF_PAD).astype(BF16) for i in range(2)],
        ffn_wu=[_pad_last(W['ffn_wu'][l, i], D_FF_PAD).astype(BF16) for i in range(2)],
        ffn_wd=[jnp.pad(W['ffn_wd'][l, i], ((0, D_FF_PAD - D_FF), (0, 0))).astype(BF16) for i in range(2)],
        ln_g=[row(W['ln_g'][l, i]) for i in range(3)],
        ln_b=[row(W['ln_b'][l, i]) for i in range(3)],
        w_in=jnp.concatenate([w_rw, w_gla, w_ml], axis=1).astype(BF16),
        w_o_rw=w_out[:RW_W].astype(BF16), w_o_gla=w_o_gla.astype(BF16),
        w_o_ml=w_out[RW_W + gla_w:].astype(BF16),
        rw=(row(W['rw_mu'][l]), row(W['rw_w0'][l]), row(W['rw_a0'][l]), w_lora.astype(BF16),
            W['rw_g2'][l].astype(BF16), row(W['rw_k_k'][l]), row(W['rw_k_a'][l]), row(W['rw_r_k'][l]),
            row(W['rw_gn_g'][l]), row(W['rw_gn_b'][l]),
            jnp.kron(jnp.eye(RW_HEADS, dtype=F32), jnp.ones((RW_HD, RW_HD), F32)).astype(BF16)),
        gla=(a2.astype(BF16), row(_pad_heads(W['gla_a_b'][l], GLA_HEADS, GLA_DK, GLA_DK_PAD)),
             row(_pad_heads(W['gla_gn_g'][l], GLA_HEADS, GLA_DV, GLA_DV_PAD))),
        ml=(W['ml_conv_w'][l], row(W['ml_conv_b'][l]), row(_pad_last(W['ml_i_b'][l], 128)),
            row(_pad_last(W['ml_f_b'][l], 128)), row(W['ml_gn_g'][l])),
    )


def _layer(x, mods, st, P, bb, tt, L):
    B, T, _ = x.shape
    st_rw, st_shift, st_gla, st_c, st_n, st_m, st_conv = st
    x = _ffn_call(x, mods, 0, P['ffn_wg'][0], P['ffn_wu'][0], P['ffn_wd'][0], P['ln_g'][0], P['ln_b'][0], bb, tt)
    p_rw, p_gla, p_ml = _inproj_call(x, mods, P['w_in'], bb, tt)
    o_rw, new_rw = _rwkv_call(p_rw, st_shift, st_rw, P['rw'], L)
    gla_s0 = jnp.pad(st_gla, ((0, 0), (0, 0), (0, GLA_DK_PAD - GLA_DK), (0, GLA_DV_PAD - GLA_DV)))
    o_gla, new_gla = _gla_call(p_gla, gla_s0, P['gla'], L)
    conv0 = jnp.pad(st_conv, ((0, 0), (8 - (ML_CONV - 1), 0), (0, 0)))
    m0 = _pad_last(st_m, 128).reshape(B, 1, 128)
    o_ml, new_c, new_n, new_m = _mlstm_call(p_ml, conv0, st_c, st_n, m0, P['ml'], L)
    x = _outproj_call(x, mods, o_rw, o_gla, o_ml, P['w_o_rw'], P['w_o_gla'], P['w_o_ml'],
                      P['ln_g'][1], P['ln_b'][1], bb, tt)
    x = _ffn_call(x, mods, 2, P['ffn_wg'][1], P['ffn_wu'][1], P['ffn_wd'][1], P['ln_g'][2], P['ln_b'][2], bb, tt)
    new = (new_rw, p_rw[:, -1, :], new_gla[:, :, :GLA_DK, :GLA_DV], new_c, new_n,
           new_m[:, 0, :ML_HEADS], p_ml[:, T - (ML_CONV - 1):, :2 * ML_W])
    return x, new


def _run_trunk(x, mods_all, states, layers, bb, tt):
    B, T, _ = x.shape
    L = math.gcd(T, MIX_CHUNK)
    collected = [[] for _ in states]
    for l in range(DEPTH):
        x, new = _layer(x, mods_all[l], tuple(s[l] for s in states), layers[l], bb, tt, L)
        for lst, s in zip(collected, new):
            lst.append(s)
    return x, tuple(jnp.stack(lst) for lst in collected)


def kernel(x_prompt, x_sample, c_prompt, c_sample, state_rwkv, state_rwkv_shift, state_gla, state_mlstm_c, state_mlstm_n, state_mlstm_m, state_mlstm_conv, ada_w, ada_b, ln_g, ln_b, ffn_wg, ffn_wu, ffn_wd, w_in, w_out, rw_mu, rw_w0, rw_w2, rw_a0, rw_a2, rw_g2, rw_k_k, rw_k_a, rw_r_k, rw_gn_g, rw_gn_b, gla_a2, gla_a_b, gla_gn_g, ml_conv_w, ml_conv_b, ml_i_b, ml_f_b, ml_gn_g):
    W = dict(ln_g=ln_g, ln_b=ln_b, ffn_wg=ffn_wg, ffn_wu=ffn_wu, ffn_wd=ffn_wd, w_in=w_in, w_out=w_out,
             rw_mu=rw_mu, rw_w0=rw_w0, rw_w2=rw_w2, rw_a0=rw_a0, rw_a2=rw_a2, rw_g2=rw_g2, rw_k_k=rw_k_k,
             rw_k_a=rw_k_a, rw_r_k=rw_r_k.reshape(DEPTH, RW_W), rw_gn_g=rw_gn_g, rw_gn_b=rw_gn_b,
             gla_a2=gla_a2, gla_a_b=gla_a_b, gla_gn_g=gla_gn_g, ml_conv_w=ml_conv_w, ml_conv_b=ml_conv_b,
             ml_i_b=ml_i_b, ml_f_b=ml_f_b, ml_gn_g=ml_gn_g)
    layers = [_prep_layer(l, W) for l in range(DEPTH)]
    Bp, Tp, _ = x_prompt.shape
    Bs, Ts, _ = x_sample.shape

    mods = _ada_call(jnp.concatenate([c_prompt, c_sample], axis=0), ada_w, ada_b)
    mods_p = mods[:, :Bp].reshape(DEPTH, Bp, 9, 1, D_MODEL)
    mods_s = mods[:, Bp:].reshape(DEPTH, Bs, 9, 1, D_MODEL)

    dt = x_prompt.dtype
    zero_states = (
        jnp.zeros((DEPTH, Bp, RW_HEADS, RW_HD, RW_HD), dt),
        jnp.zeros((DEPTH, Bp, RW_PROJ), dt),
        jnp.zeros((DEPTH, Bp, GLA_HEADS, GLA_DK, GLA_DV), dt),
        jnp.zeros((DEPTH, Bp, ML_HEADS, ML_HD, ML_HD), dt),
        jnp.zeros((DEPTH, Bp, ML_HEADS, ML_HD), dt),
        jnp.zeros((DEPTH, Bp, ML_HEADS), dt),
        jnp.zeros((DEPTH, Bp, ML_CONV - 1, 2 * ML_W), dt),
    )
    y_prompt, p_states = _run_trunk(x_prompt, mods_p, zero_states, layers, 1, 512)
    sample_states = (state_rwkv, state_rwkv_shift, state_gla, state_mlstm_c, state_mlstm_n,
                     state_mlstm_m, state_mlstm_conv)
    y_sample, s_states = _run_trunk(x_sample, mods_s, sample_states, layers, 512 // Ts, Ts)
    return (y_prompt, y_sample) + p_states + s_states
```

```python
import functools
import math

import jax
import jax.numpy as jnp
from jax import lax
from jax.experimental import pallas as pl
from jax.experimental.pallas import tpu as pltpu

F32 = jnp.float32
BF16 = jnp.bfloat16

D_MODEL = 1024
DEPTH = 4
D_FF = 2752
D_FF_PAD = 2816
FF_CHUNKS = 2

RW_HEADS, RW_HD = 6, 64
RW_W = RW_HEADS * RW_HD
RW_PROJ = 3 * RW_W + 64 + 64 + 128
RW_GN_EPS = 64e-5

GLA_HEADS, GLA_DK, GLA_DV = 4, 48, 96
GLA_DK_PAD, GLA_DV_PAD = 64, 128
GLA_QK_PAD = GLA_HEADS * GLA_DK_PAD
GLA_W_PAD = GLA_HEADS * GLA_DV_PAD
GLA_LORA = 16
GLA_PROJ_PAD = 2 * GLA_QK_PAD + 2 * GLA_W_PAD + 128
GLA_TAU = 16.0

ML_HEADS, ML_HD = 4, 64
ML_W = ML_HEADS * ML_HD
ML_CONV = 4
ML_PROJ_PAD = 4 * ML_W + 2 * 128

MIX_CHUNK = 64
ALPHA = (2 * DEPTH) ** 0.25
LN_EPS = 1e-5
NORM_EPS = 1e-6

VMEM_LIMIT = 56 * 1024 * 1024

_NN = (((1,), (0,)), ((), ()))
_NT = (((1,), (1,)), ((), ()))
_TN = (((0,), (0,)), ((), ()))


def _dg(a, b, dims=_NN):
    return lax.dot_general(a, b, dims, preferred_element_type=F32)


def _split2(x):
    hi = x.astype(BF16)
    lo = (x - hi.astype(F32)).astype(BF16)
    return hi, lo


def _split3(x):
    x1 = x.astype(BF16)
    r1 = x - x1.astype(F32)
    x2 = r1.astype(BF16)
    x3 = (r1 - x2.astype(F32)).astype(BF16)
    return x1, x2, x3


_sp = _split2


def _d3(a_s, b_s, dims=_NN):
    (ah, al), (bh, bl) = a_s, b_s
    return _dg(ah, bh, dims) + (_dg(ah, bl, dims) + _dg(al, bh, dims))


def _dot_mask_lhs(m_bf16, x, dims=_NN):
    x1, x2, x3 = _split3(x)
    return _dg(m_bf16, x1, dims) + (_dg(m_bf16, x2, dims) + _dg(m_bf16, x3, dims))


def _dot_mask_rhs(x, m_bf16):
    xh, xl = _split2(x)
    return _dg(xh, m_bf16) + _dg(xl, m_bf16)


def _sigmoid(x):
    return 1.0 / (1.0 + jnp.exp(-x))


def _log_sigmoid(x):
    return jnp.minimum(x, 0.0) - jnp.log1p(jnp.exp(-jnp.abs(x)))


def _tri_masks(n):
    row = lax.broadcasted_iota(jnp.int32, (n, n), 0)
    col = lax.broadcasted_iota(jnp.int32, (n, n), 1)
    return row >= col, row > col, row == col


def _modulate(x3, sc_ref, sh_ref):
    return x3 * (1.0 + sc_ref[:, 0]) + sh_ref[:, 0]


def _residual_ln(x3, upd3, gt_ref, lg_ref, lb_ref, w):
    y = ALPHA * x3 + (w * (1.0 + gt_ref[:, 0])) * upd3
    mu = jnp.mean(y, -1, keepdims=True)
    yc = y - mu
    var = jnp.mean(yc * yc, -1, keepdims=True)
    return yc * lax.rsqrt(var + LN_EPS) * lg_ref[...] + lb_ref[...]


def _ada_kernel(c_ref, w_ref, b_ref, o_ref):
    c = c_ref[...]
    s = (c * _sigmoid(c)).astype(BF16)
    o_ref[0] = _dg(s, w_ref[0].astype(BF16)) + b_ref[0]


def _ada_call(c_all, ada_w, ada_b):
    nb = c_all.shape[0]
    tn = 1536
    return pl.pallas_call(
        _ada_kernel,
        grid=(DEPTH, 9 * D_MODEL // tn),
        in_specs=[
            pl.BlockSpec((nb, D_MODEL), lambda l, j: (0, 0)),
            pl.BlockSpec((1, D_MODEL, tn), lambda l, j: (l, 0, j)),
            pl.BlockSpec((1, 1, tn), lambda l, j: (l, 0, j)),
        ],
        out_specs=pl.BlockSpec((1, nb, tn), lambda l, j: (l, 0, j)),
        out_shape=jax.ShapeDtypeStruct((DEPTH, nb, 9 * D_MODEL), F32),
        compiler_params=pltpu.CompilerParams(
            dimension_semantics=("parallel", "parallel"), vmem_limit_bytes=VMEM_LIMIT),
        name="ada_mods",
    )(c_all, ada_w, ada_b.reshape(DEPTH, 1, 9 * D_MODEL))


def _mod_spec(bb, k):
    return pl.BlockSpec((bb, 1, 1, D_MODEL), lambda i, j, k=k: (i, k, 0, 0))


def _const_spec(shape):
    nd = len(shape)
    return pl.BlockSpec(shape, lambda i, j, nd=nd: (0,) * nd, pipeline_mode=pl.Buffered(1))


def _ffn_kernel(x_ref, sh_ref, sc_ref, gt_ref, wg_ref, wu_ref, wd_ref, lg_ref, lb_ref, o_ref):
    bb, tt, d = x_ref.shape
    x3 = x_ref[...]
    h = _modulate(x3, sc_ref, sh_ref).reshape(bb * tt, d).astype(BF16)
    ck = D_FF_PAD // FF_CHUNKS
    acc = jnp.zeros((bb * tt, d), F32)
    for c in range(FF_CHUNKS):
        g = _dg(h, wg_ref[:, c * ck:(c + 1) * ck])
        u = _dg(h, wu_ref[:, c * ck:(c + 1) * ck])
        a = (g * _sigmoid(g) * u).astype(BF16)
        acc = acc + _dg(a, wd_ref[c * ck:(c + 1) * ck, :])
    o_ref[...] = _residual_ln(x3, acc.reshape(bb, tt, d), gt_ref, lg_ref, lb_ref, 0.5)


def _ffn_call(x, mods, sub, wg, wu, wd, lg, lb, bb, tt):
    B, T, d = x.shape
    return pl.pallas_call(
        _ffn_kernel,
        grid=(B // bb, T // tt),
        in_specs=[
            pl.BlockSpec((bb, tt, d), lambda i, j: (i, j, 0)),
            _mod_spec(bb, 3 * sub), _mod_spec(bb, 3 * sub + 1), _mod_spec(bb, 3 * sub + 2),
            _const_spec((d, D_FF_PAD)), _const_spec((d, D_FF_PAD)), _const_spec((D_FF_PAD, d)),
            _const_spec((1, d)), _const_spec((1, d)),
        ],
        out_specs=pl.BlockSpec((bb, tt, d), lambda i, j: (i, j, 0)),
        out_shape=jax.ShapeDtypeStruct((B, T, d), F32),
        compiler_params=pltpu.CompilerParams(
            dimension_semantics=("parallel", "parallel"), vmem_limit_bytes=VMEM_LIMIT),
        name="ffn_sublayer",
    )(x, mods, mods, mods, wg, wu, wd, lg, lb)


def _inproj_kernel(x_ref, sh_ref, sc_ref, w_ref, orw_ref, ogla_ref, oml_ref):
    bb, tt, d = x_ref.shape
    h = _modulate(x_ref[...], sc_ref, sh_ref).reshape(bb * tt, d).astype(BF16)
    o1, o2 = RW_PROJ, RW_PROJ + GLA_PROJ_PAD
    orw_ref[...] = _dg(h, w_ref[:, :o1]).reshape(bb, tt, RW_PROJ)
    ogla_ref[...] = _dg(h, w_ref[:, o1:o2]).reshape(bb, tt, GLA_PROJ_PAD)
    oml_ref[...] = _dg(h, w_ref[:, o2:]).reshape(bb, tt, ML_PROJ_PAD)


def _inproj_call(x, mods, w_in, bb, tt):
    B, T, d = x.shape
    widths = (RW_PROJ, GLA_PROJ_PAD, ML_PROJ_PAD)
    return pl.pallas_call(
        _inproj_kernel,
        grid=(B // bb, T // tt),
        in_specs=[
            pl.BlockSpec((bb, tt, d), lambda i, j: (i, j, 0)),
            _mod_spec(bb, 3), _mod_spec(bb, 4),
            _const_spec((d, sum(widths))),
        ],
        out_specs=[pl.BlockSpec((bb, tt, w), lambda i, j: (i, j, 0)) for w in widths],
        out_shape=[jax.ShapeDtypeStruct((B, T, w), F32) for w in widths],
        compiler_params=pltpu.CompilerParams(
            dimension_semantics=("parallel", "parallel"), vmem_limit_bytes=VMEM_LIMIT),
        name="mixer_in_proj",
    )(x, mods, mods, w_in)


def _outproj_kernel(x_ref, gt_ref, orw_ref, ogla_ref, oml_ref, w1_ref, w2_ref, w3_ref,
                    lg_ref, lb_ref, o_ref):
    bb, tt, d = x_ref.shape
    flat = lambda ref: ref[...].reshape(bb * tt, ref.shape[-1]).astype(BF16)
    mix = _dg(flat(orw_ref), w1_ref[...]) + _dg(flat(ogla_ref), w2_ref[...]) + _dg(flat(oml_ref), w3_ref[...])
    o_ref[...] = _residual_ln(x_ref[...], mix.reshape(bb, tt, d), gt_ref, lg_ref, lb_ref, 1.0)


def _outproj_call(x, mods, o_rw, o_gla, o_ml, w1, w2, w3, lg, lb, bb, tt):
    B, T, d = x.shape
    tok = lambda w: pl.BlockSpec((bb, tt, w), lambda i, j: (i, j, 0))
    return pl.pallas_call(
        _outproj_kernel,
        grid=(B // bb, T // tt),
        in_specs=[
            tok(d), _mod_spec(bb, 5), tok(RW_W), tok(GLA_W_PAD), tok(ML_W),
            _const_spec((RW_W, d)), _const_spec((GLA_W_PAD, d)), _const_spec((ML_W, d)),
            _const_spec((1, d)), _const_spec((1, d)),
        ],
        out_specs=tok(d),
        out_shape=jax.ShapeDtypeStruct((B, T, d), F32),
        compiler_params=pltpu.CompilerParams(
            dimension_semantics=("parallel", "parallel"), vmem_limit_bytes=VMEM_LIMIT),
        name="mixer_out_proj",
    )(x, mods, o_rw, o_gla, o_ml, w1, w2, w3, lg, lb)


def _block_tri(n, L):
    row = lax.broadcasted_iota(jnp.int32, (n, n), 0)
    col = lax.broadcasted_iota(jnp.int32, (n, n), 1)
    incl = row >= col
    if n > L:
        sh = int(math.log2(L))
        incl = incl & (lax.shift_right_logical(row, sh) == lax.shift_right_logical(col, sh))
    return incl


def _rows3(x, bs, L):
    return x.reshape(bs, L, x.shape[-1])


def _rwkv_kernel(p_ref, sh0_ref, s0_ref, mu_ref, w0_ref, a0_ref, wl_ref, g2_ref, kk_ref, ka_ref,
                 rk_ref, gg_ref, gb_ref, seg_ref, o_ref, sT_ref, s_scr, prev_scr):
    bs, L, _ = p_ref.shape
    n = bs * L
    ci = pl.program_id(1)

    @pl.when(ci == 0)
    def _():
        s_scr[...] = s0_ref[...]
        prev_scr[...] = sh0_ref[...]

    p3 = p_ref[...]
    rolled = _rows3(pltpu.roll(p3.reshape(n, RW_PROJ), 1, 0), bs, L)
    row = lax.broadcasted_iota(jnp.int32, (1, L, 1), 1)
    prev = jnp.where(row == 0, prev_scr[...], rolled)
    prev_scr[...] = p3[:, L - 1:L, :]
    xs = (p3 + (prev - p3) * mu_ref[...]).reshape(n, RW_PROJ)

    r = xs[:, 0:RW_W]
    k = xs[:, RW_W:2 * RW_W]
    v = xs[:, 2 * RW_W:3 * RW_W]
    lin = xs[:, 3 * RW_W:3 * RW_W + 128]
    lane = lax.broadcasted_iota(jnp.int32, (1, 128), 1)
    lin = jnp.where(lane < 64, jnp.tanh(lin), lin)
    dl = _dg(lin.astype(BF16), wl_ref[...])
    lw = -jnp.exp(_log_sigmoid(w0_ref[...] + dl[:, :RW_W]) - 0.5)
    a_sig = _sigmoid(a0_ref[...] + dl[:, RW_W:])
    g = _dg(_sigmoid(xs[:, 3 * RW_W + 128:]).astype(BF16), g2_ref[...])

    seg = seg_ref[...]
    kk = k * kk_ref[...]
    kk = kk * lax.rsqrt(jnp.maximum(_dot_mask_rhs(kk * kk, seg), 1e-24))
    k2 = k * (1.0 + (a_sig - 1.0) * ka_ref[...])
    bv = kk * a_sig

    c = _dot_mask_lhs(_block_tri(n, L).astype(BF16), lw)
    c3 = _rows3(c, bs, L)
    cl3 = c3[:, L - 1:L, :]
    einv = jnp.exp(-c)
    edl = jnp.exp(cl3 - c3).reshape(n, RW_W)
    a_t = -kk * jnp.exp(c - lw)
    r_t = r * jnp.exp(c)
    k_h = k2 * einv
    b_h = bv * einv
    k_e = k2 * edl
    b_e = bv * edl
    p_end = jnp.exp(cl3)

    incl, strict, eye = _tri_masks(L)
    units = [(b, h) for b in range(bs) for h in range(RW_HEADS)]
    blk = lambda t, b, h: t[b * L:(b + 1) * L, h * RW_HD:(h + 1) * RW_HD]
    U = lambda f: [f(b, h) for b, h in units]
    idx = range(len(units))

    at_s = U(lambda b, h: _sp(blk(a_t, b, h)))
    rt_s = U(lambda b, h: _sp(blk(r_t, b, h)))
    ar_s = [tuple(jnp.concatenate([x, y], axis=0) for x, y in zip(at_s[i], rt_s[i])) for i in idx]
    kh_s = U(lambda b, h: _sp(blk(k_h, b, h)))
    bh_s = U(lambda b, h: _sp(blk(b_h, b, h)))
    v_s = U(lambda b, h: _sp(blk(v, b, h)))
    gk = [_d3(ar_s[i], kh_s[i], _NT) for i in idx]
    gb = [_d3(ar_s[i], bh_s[i], _NT) for i in idx]
    a_ak = [_sp(jnp.where(strict, t[:L], 0.0)) for t in gk]
    a_rk = [_sp(jnp.where(incl, t[L:], 0.0)) for t in gk]
    a_ab = [jnp.where(strict, t[:L], 0.0) for t in gb]
    a_rb = [_sp(jnp.where(incl, t[L:], 0.0)) for t in gb]
    t_inv = [jnp.where(eye, 1.0, t) for t in a_ab]
    pw = a_ab
    for _ in range(int(math.log2(L)) - 1):
        pw_s = [_sp(t) for t in pw]
        pw = [_d3(t, t) for t in pw_s]
        pw_s = [_sp(t) for t in pw]
        t_inv = [t_inv[i] + _d3(_sp(t_inv[i]), pw_s[i]) for i in idx]
    s0 = U(lambda b, h: s_scr[b, h])
    s0_s = [_sp(t) for t in s0]
    x = [_d3(at_s[i], s0_s[i], _NT) + _d3(a_ak[i], v_s[i]) for i in idx]
    u_s = [_sp(_d3(_sp(t_inv[i]), _sp(x[i]))) for i in idx]
    ys = [_d3(rt_s[i], s0_s[i], _NT) + _d3(a_rk[i], v_s[i]) + _d3(a_rb[i], u_s[i]) for i in idx]
    ke_s = U(lambda b, h: _sp(blk(k_e, b, h)))
    be_s = U(lambda b, h: _sp(blk(b_e, b, h)))
    for i, (b, h) in enumerate(units):
        s_scr[b, h] = (s0[i] * p_end[b, :, h * RW_HD:(h + 1) * RW_HD]
                       + _d3(v_s[i], ke_s[i], _TN) + _d3(u_s[i], be_s[i], _TN))
    y = jnp.concatenate(
        [jnp.concatenate(ys[b * RW_HEADS:(b + 1) * RW_HEADS], axis=-1) for b in range(bs)], axis=0)

    inv_hd = 1.0 / RW_HD
    yc = y - _dot_mask_rhs(y, seg) * inv_hd
    var = _dot_mask_rhs(yc * yc, seg) * inv_hd
    yn = yc * lax.rsqrt(var + RW_GN_EPS) * gg_ref[...] + gb_ref[...]
    bonus = _dot_mask_rhs(r * k2 * rk_ref[...], seg) * v
    o_ref[...] = _rows3((yn + bonus) * g, bs, L)

    @pl.when(ci == pl.num_programs(1) - 1)
    def _():
        sT_ref[...] = s_scr[...]


def _seq_spec(bs, shape):
    nd = len(shape)
    return pl.BlockSpec((bs,) + shape, lambda b, c, nd=nd: (b,) + (0,) * nd)


def _tok_spec(bs, L, w):
    return pl.BlockSpec((bs, L, w), lambda b, c: (b, c, 0))


def _param_spec(shape):
    nd = len(shape)
    return pl.BlockSpec(shape, lambda b, c, nd=nd: (0,) * nd)


def _mixer_params():
    return pltpu.CompilerParams(dimension_semantics=("parallel", "arbitrary"), vmem_limit_bytes=VMEM_LIMIT)


def _rwkv_call(p, shift0, s0, prm, L, bs):
    B, T, _ = p.shape
    mu, w0, a0, wl, g2, k_k, k_a, r_k, gn_g, gn_b, seg = prm
    row = lambda w: _param_spec((1, w))
    st = (RW_HEADS, RW_HD, RW_HD)
    return pl.pallas_call(
        _rwkv_kernel,
        grid=(B // bs, T // L),
        in_specs=[
            _tok_spec(bs, L, RW_PROJ), _seq_spec(bs, (1, RW_PROJ)), _seq_spec(bs, st),
            row(RW_PROJ), row(RW_W), row(RW_W), _param_spec((128, 2 * RW_W)), _param_spec((128, RW_W)),
            row(RW_W), row(RW_W), row(RW_W), row(RW_W), row(RW_W), _param_spec((RW_W, RW_W)),
        ],
        out_specs=[_tok_spec(bs, L, RW_W), _seq_spec(bs, st)],
        out_shape=[jax.ShapeDtypeStruct((B, T, RW_W), F32), jax.ShapeDtypeStruct((B,) + st, F32)],
        scratch_shapes=[pltpu.VMEM((bs,) + st, F32), pltpu.VMEM((bs, 1, RW_PROJ), F32)],
        compiler_params=_mixer_params(),
        name="rwkv7_mixer",
    )(p, shift0.reshape(B, 1, RW_PROJ), s0, mu, w0, a0, wl, g2, k_k, k_a, r_k, gn_g, gn_b, seg)


def _gla_kernel(p_ref, s0_ref, a2_ref, ab_ref, gg_ref, o_ref, sT_ref, s_scr):
    bs, L, _ = p_ref.shape
    n = bs * L
    ci = pl.program_id(1)

    @pl.when(ci == 0)
    def _():
        s_scr[...] = s0_ref[...]

    p = p_ref[...].reshape(n, GLA_PROJ_PAD)
    q = p[:, 0:GLA_QK_PAD]
    k = p[:, GLA_QK_PAD:2 * GLA_QK_PAD]
    v = p[:, 2 * GLA_QK_PAD:2 * GLA_QK_PAD + GLA_W_PAD]
    gr = p[:, 2 * GLA_QK_PAD + GLA_W_PAD:2 * GLA_QK_PAD + 2 * GLA_W_PAD]
    gd = p[:, 2 * GLA_QK_PAD + 2 * GLA_W_PAD:]
    log_a = _log_sigmoid(_dg(gd.astype(BF16), a2_ref[...]) + ab_ref[...]) * (1.0 / GLA_TAU)

    bc = _dot_mask_lhs(_block_tri(n, L).astype(BF16), log_a)
    bc3 = _rows3(bc, bs, L)
    bl3 = bc3[:, L - 1:L, :]
    qd = q * (GLA_DK ** -0.5) * jnp.exp(bc)
    kd = k * jnp.exp(-bc)
    ks = k * jnp.exp(bl3 - bc3).reshape(n, GLA_QK_PAD)
    d_end = jnp.exp(bl3)

    incl, _, _ = _tri_masks(L)
    _, _, eye_k = _tri_masks(GLA_DK_PAD)
    units = [(b, h) for b in range(bs) for h in range(GLA_HEADS)]
    idx = range(len(units))
    kblk = lambda t, b, h: t[b * L:(b + 1) * L, h * GLA_DK_PAD:(h + 1) * GLA_DK_PAD]
    vblk = lambda t, b, h: t[b * L:(b + 1) * L, h * GLA_DV_PAD:(h + 1) * GLA_DV_PAD]
    U = lambda f: [f(b, h) for b, h in units]

    qd_s = U(lambda b, h: _sp(kblk(qd, b, h)))
    kd_s = U(lambda b, h: _sp(kblk(kd, b, h)))
    ks_s = U(lambda b, h: _sp(kblk(ks, b, h)))
    v_s = U(lambda b, h: _sp(vblk(v, b, h)))
    a_s = [_sp(jnp.where(incl, _d3(qd_s[i], kd_s[i], _NT), 0.0)) for i in idx]
    s0_s = U(lambda b, h: _sp(s_scr[b, h]))
    o = [_d3(a_s[i], v_s[i]) + _d3(qd_s[i], s0_s[i]) for i in idx]
    dm_s = U(lambda b, h: _sp(jnp.where(eye_k, jnp.broadcast_to(
        d_end[b, :, h * GLA_DK_PAD:(h + 1) * GLA_DK_PAD], (GLA_DK_PAD, GLA_DK_PAD)), 0.0)))
    for i, (b, h) in enumerate(units):
        s_scr[b, h] = _d3(dm_s[i], s0_s[i]) + _d3(ks_s[i], v_s[i], _TN)
    outs = []
    for i, (b, h) in enumerate(units):
        ms = jnp.sum(o[i] * o[i], -1, keepdims=True) * (1.0 / GLA_DV)
        sv = slice(h * GLA_DV_PAD, (h + 1) * GLA_DV_PAD)
        grh = vblk(gr, b, h)
        outs.append(o[i] * lax.rsqrt(ms + NORM_EPS) * gg_ref[:, sv] * (grh * _sigmoid(grh)))
    out = jnp.concatenate(
        [jnp.concatenate(outs[b * GLA_HEADS:(b + 1) * GLA_HEADS], axis=-1) for b in range(bs)], axis=0)
    o_ref[...] = _rows3(out, bs, L)

    @pl.when(ci == pl.num_programs(1) - 1)
    def _():
        sT_ref[...] = s_scr[...]


def _gla_call(p, s0, prm, L, bs):
    B, T, _ = p.shape
    a2, a_b, gn_g = prm
    st = (GLA_HEADS, GLA_DK_PAD, GLA_DV_PAD)
    return pl.pallas_call(
        _gla_kernel,
        grid=(B // bs, T // L),
        in_specs=[
            _tok_spec(bs, L, GLA_PROJ_PAD), _seq_spec(bs, st),
            _param_spec((128, GLA_QK_PAD)), _param_spec((1, GLA_QK_PAD)), _param_spec((1, GLA_W_PAD)),
        ],
        out_specs=[_tok_spec(bs, L, GLA_W_PAD), _seq_spec(bs, st)],
        out_shape=[jax.ShapeDtypeStruct((B, T, GLA_W_PAD), F32), jax.ShapeDtypeStruct((B,) + st, F32)],
        scratch_shapes=[pltpu.VMEM((bs,) + st, F32)],
        compiler_params=_mixer_params(),
        name="gla_mixer",
    )(p, s0, a2, a_b, gn_g)


def _mlstm_kernel(p_ref, cv0_ref, c0_ref, n0_ref, m0_ref, cw_ref, cb_ref, ib_ref, fb_ref, gg_ref,
                  o_ref, cT_ref, nT_ref, mT_ref, c_scr, n_scr, m_scr, cv_scr):
    bs, L, _ = p_ref.shape
    n = bs * L
    ci = pl.program_id(1)

    @pl.when(ci == 0)
    def _():
        c_scr[...] = c0_ref[...]
        n_scr[...] = n0_ref[...]
        m_scr[...] = m0_ref[...]
        cv_scr[...] = cv0_ref[...]

    p = p_ref[...].reshape(n, ML_PROJ_PAD)
    x = p[:, 0:2 * ML_W]
    row = lax.broadcasted_iota(jnp.int32, (1, L, 1), 1)
    conv = cb_ref[...] + x * cw_ref[ML_CONV - 1:ML_CONV, :]
    for s in range(1, ML_CONV):
        cr = jnp.concatenate([pltpu.roll(cv_scr[b], s, 0) for b in range(bs)], axis=0)
        if L > 8:
            cr = jnp.concatenate([cr] * (L // 8), axis=0)
        shifted = jnp.where(row < s, _rows3(cr, bs, L), _rows3(pltpu.roll(x, s, 0), bs, L))
        conv = conv + shifted.reshape(n, 2 * ML_W) * cw_ref[ML_CONV - 1 - s:ML_CONV - s, :]
    cv_scr[...] = _rows3(x, bs, L)[:, L - 8:L, :]
    qk = conv * _sigmoid(conv)
    q = qk[:, :ML_W]
    k = qk[:, ML_W:] * (ML_HD ** -0.5)
    v = p[:, 2 * ML_W:3 * ML_W]
    o_raw = p[:, 3 * ML_W:4 * ML_W]
    gi = p[:, 4 * ML_W:4 * ML_W + 128] + ib_ref[...]
    lf = _log_sigmoid(p[:, 4 * ML_W + 128:] + fb_ref[...])

    bcum = _dot_mask_lhs(_block_tri(n, L).astype(BF16), lf)
    b3 = _rows3(bcum, bs, L)
    bl3 = b3[:, L - 1:L, :]
    m_prev3 = m_scr[...]
    inter = (b3 + m_prev3).reshape(n, 128)
    d = gi - bcum
    lane = lax.broadcasted_iota(jnp.int32, (1, 128), 1)
    ones_l = jnp.ones((L, 128), BF16)
    incl, _, _ = _tri_masks(L)

    units = [(b, h) for b in range(bs) for h in range(ML_HEADS)]
    idx = range(len(units))
    U = lambda f: [f(b, h) for b, h in units]
    rows = lambda t, b: t[b * L:(b + 1) * L]
    blk = lambda t, b, h: t[b * L:(b + 1) * L, h * ML_HD:(h + 1) * ML_HD]
    pick = lambda t, h: jnp.sum(jnp.where(lane == h, t, 0.0), -1, keepdims=True)

    b_col = U(lambda b, h: pick(rows(bcum, b), h))
    gi_col = U(lambda b, h: pick(rows(gi, b), h))
    inter_col = U(lambda b, h: pick(rows(inter, b), h))
    bl_h = U(lambda b, h: pick(bl3[b], h))
    mp_h = U(lambda b, h: pick(m_prev3[b], h))
    d_row = U(lambda b, h: _dot_mask_lhs(ones_l, jnp.where(lane == h, rows(d, b), 0.0), _NT))
    dmat = [jnp.where(incl, b_col[i] + d_row[i], -jnp.inf) for i in idx]
    m_t = [jnp.maximum(inter_col[i], jnp.max(dmat[i], -1, keepdims=True)) for i in idx]
    dw = [jnp.exp(dmat[i] - m_t[i]) for i in idx]
    w_int = [jnp.exp(inter_col[i] - m_t[i]) for i in idx]
    q_s = U(lambda b, h: _sp(blk(q, b, h)))
    k_s = U(lambda b, h: _sp(blk(k, b, h)))
    v_s = U(lambda b, h: _sp(blk(v, b, h)))
    s = [_d3(q_s[i], k_s[i], _NT) * dw[i] for i in idx]
    cmat = U(lambda b, h: c_scr[b, h])
    n_row = U(lambda b, h: n_scr[b, h:h + 1, :])
    num = [_d3(_sp(s[i]), v_s[i]) + w_int[i] * _d3(q_s[i], _sp(cmat[i]), _NT) for i in idx]
    den = [jnp.sum(s[i], -1, keepdims=True)
           + w_int[i] * jnp.sum(blk(q, *units[i]) * n_row[i], -1, keepdims=True) for i in idx]
    hc = [num[i] / jnp.maximum(jnp.abs(den[i]), jnp.exp(-m_t[i])) for i in idx]
    m_last = [t[L - 1:L, :] for t in m_t]
    w_state = [jnp.exp(bl_h[i] + mp_h[i] - m_last[i]) for i in idx]
    w_j = [jnp.exp(bl_h[i] - b_col[i] + gi_col[i] - m_last[i]) for i in idx]
    for i, (b, h) in enumerate(units):
        kh, vh = blk(k, b, h), blk(v, b, h)
        c_scr[b, h] = cmat[i] * w_state[i] + _d3(_sp(w_j[i] * vh), k_s[i], _TN)
        n_scr[b, h:h + 1, :] = n_row[i] * w_state[i] + jnp.sum(w_j[i] * kh, 0, keepdims=True)
    for b in range(bs):
        m_new = m_prev3[b]
        for h in range(ML_HEADS):
            m_new = jnp.where(lane == h, m_last[b * ML_HEADS + h], m_new)
        m_scr[b] = m_new
    outs = []
    for i, (b, h) in enumerate(units):
        mu = jnp.mean(hc[i], -1, keepdims=True)
        hcc = hc[i] - mu
        var = jnp.mean(hcc * hcc, -1, keepdims=True)
        outs.append(hcc * lax.rsqrt(var + NORM_EPS) * gg_ref[:, h * ML_HD:(h + 1) * ML_HD])
    out = jnp.concatenate(
        [jnp.concatenate(outs[b * ML_HEADS:(b + 1) * ML_HEADS], axis=-1) for b in range(bs)], axis=0)
    o_ref[...] = _rows3(out * _sigmoid(o_raw), bs, L)

    @pl.when(ci == pl.num_programs(1) - 1)
    def _():
        cT_ref[...] = c_scr[...]
        nT_ref[...] = n_scr[...]
        mT_ref[...] = m_scr[...]


def _mlstm_call(p, conv0, c0, n0, m0, prm, L, bs):
    B, T, _ = p.shape
    assert bs == 1 or L == 8
    cw, cb, ib, fb, gn_g = prm
    st_c, st_n, st_m, st_cv = (ML_HEADS, ML_HD, ML_HD), (ML_HEADS, ML_HD), (1, 128), (8, 2 * ML_W)
    return pl.pallas_call(
        _mlstm_kernel,
        grid=(B // bs, T // L),
        in_specs=[
            _tok_spec(bs, L, ML_PROJ_PAD),
            _seq_spec(bs, st_cv), _seq_spec(bs, st_c), _seq_spec(bs, st_n), _seq_spec(bs, st_m),
            _param_spec((ML_CONV, 2 * ML_W)), _param_spec((1, 2 * ML_W)), _param_spec((1, 128)),
            _param_spec((1, 128)), _param_spec((1, ML_W)),
        ],
        out_specs=[_tok_spec(bs, L, ML_W), _seq_spec(bs, st_c), _seq_spec(bs, st_n), _seq_spec(bs, st_m)],
        out_shape=[jax.ShapeDtypeStruct((B, T, ML_W), F32),
                   jax.ShapeDtypeStruct((B,) + st_c, F32),
                   jax.ShapeDtypeStruct((B,) + st_n, F32),
                   jax.ShapeDtypeStruct((B,) + st_m, F32)],
        scratch_shapes=[pltpu.VMEM((bs,) + st_c, F32), pltpu.VMEM((bs,) + st_n, F32),
                        pltpu.VMEM((bs,) + st_m, F32), pltpu.VMEM((bs,) + st_cv, F32)],
        compiler_params=_mixer_params(),
        name="mlstm_mixer",
    )(p, conv0, c0, n0, m0, cw, cb, ib, fb, gn_g)


def _pad_heads(w, heads, width, padded):
    lead = w.shape[:-1]
    w = w.reshape(lead + (heads, width))
    w = jnp.pad(w, [(0, 0)] * len(lead) + [(0, 0), (0, padded - width)])
    return w.reshape(lead + (heads * padded,))


def _pad_last(w, n):
    return jnp.pad(w, [(0, 0)] * (w.ndim - 1) + [(0, n - w.shape[-1])])


def _prep_layer(l, W):
    gla_qk, gla_w = GLA_HEADS * GLA_DK, GLA_HEADS * GLA_DV
    w_in = W['w_in'][l]
    w_rw = w_in[:, :RW_PROJ]
    g = w_in[:, RW_PROJ:RW_PROJ + 2 * gla_qk + 2 * gla_w + GLA_LORA]
    m = w_in[:, RW_PROJ + 2 * gla_qk + 2 * gla_w + GLA_LORA:]
    g_q, g_k = g[:, :gla_qk], g[:, gla_qk:2 * gla_qk]
    g_v = g[:, 2 * gla_qk:2 * gla_qk + gla_w]
    g_gd = g[:, 2 * gla_qk + gla_w:2 * gla_qk + gla_w + GLA_LORA]
    g_gr = g[:, 2 * gla_qk + gla_w + GLA_LORA:]
    w_gla = jnp.concatenate([
        _pad_heads(g_q, GLA_HEADS, GLA_DK, GLA_DK_PAD), _pad_heads(g_k, GLA_HEADS, GLA_DK, GLA_DK_PAD),
        _pad_heads(g_v, GLA_HEADS, GLA_DV, GLA_DV_PAD), _pad_heads(g_gr, GLA_HEADS, GLA_DV, GLA_DV_PAD),
        _pad_last(g_gd, 128)], axis=1)
    w_ml = jnp.concatenate([m[:, :4 * ML_W], _pad_last(m[:, 4 * ML_W:4 * ML_W + ML_HEADS], 128),
                            _pad_last(m[:, 4 * ML_W + ML_HEADS:], 128)], axis=1)
    w_out = W['w_out'][l]
    w_o_gla = w_out[RW_W:RW_W + gla_w].reshape(GLA_HEADS, GLA_DV, D_MODEL)
    w_o_gla = jnp.pad(w_o_gla, ((0, 0), (0, GLA_DV_PAD - GLA_DV), (0, 0))).reshape(GLA_W_PAD, D_MODEL)

    zeros64 = jnp.zeros((64, RW_W), F32)
    w_lora = jnp.concatenate([jnp.concatenate([W['rw_w2'][l], zeros64], 1),
                              jnp.concatenate([zeros64, W['rw_a2'][l]], 1)], 0)
    a2 = jnp.pad(_pad_heads(W['gla_a2'][l], GLA_HEADS, GLA_DK, GLA_DK_PAD), ((0, 128 - GLA_LORA), (0, 0)))
    row = lambda t: t.reshape(1, -1)
    return dict(
        ffn_wg=[_pad_last(W['ffn_wg'][l, i], D_FF_PAD).astype(BF16) for i in range(2)],
        ffn_wu=[_pad_last(W['ffn_wu'][l, i], D_FF_PAD).astype(BF16) for i in range(2)],
        ffn_wd=[jnp.pad(W['ffn_wd'][l, i], ((0, D_FF_PAD - D_FF), (0, 0))).astype(BF16) for i in range(2)],
        ln_g=[row(W['ln_g'][l, i]) for i in range(3)],
        ln_b=[row(W['ln_b'][l, i]) for i in range(3)],
        w_in=jnp.concatenate([w_rw, w_gla, w_ml], axis=1).astype(BF16),
        w_o_rw=w_out[:RW_W].astype(BF16), w_o_gla=w_o_gla.astype(BF16),
        w_o_ml=w_out[RW_W + gla_w:].astype(BF16),
        rw=(row(W['rw_mu'][l]), row(W['rw_w0'][l]), row(W['rw_a0'][l]), w_lora.astype(BF16),
            W['rw_g2'][l].astype(BF16), row(W['rw_k_k'][l]), row(W['rw_k_a'][l]), row(W['rw_r_k'][l]),
            row(W['rw_gn_g'][l]), row(W['rw_gn_b'][l]),
            jnp.kron(jnp.eye(RW_HEADS, dtype=F32), jnp.ones((RW_HD, RW_HD), F32)).astype(BF16)),
        gla=(a2.astype(BF16), row(_pad_heads(W['gla_a_b'][l], GLA_HEADS, GLA_DK, GLA_DK_PAD)),
             row(_pad_heads(W['gla_gn_g'][l], GLA_HEADS, GLA_DV, GLA_DV_PAD))),
        ml=(W['ml_conv_w'][l], row(W['ml_conv_b'][l]), row(_pad_last(W['ml_i_b'][l], 128)),
            row(_pad_last(W['ml_f_b'][l], 128)), row(W['ml_gn_g'][l])),
    )


def _layer(x, mods, st, P, bb, tt, L, bs):
    B, T, _ = x.shape
    st_rw, st_shift, st_gla, st_c, st_n, st_m, st_conv = st
    x = _ffn_call(x, mods, 0, P['ffn_wg'][0], P['ffn_wu'][0], P['ffn_wd'][0], P['ln_g'][0], P['ln_b'][0], bb, tt)
    p_rw, p_gla, p_ml = _inproj_call(x, mods, P['w_in'], bb, tt)
    o_rw, new_rw = _rwkv_call(p_rw, st_shift, st_rw, P['rw'], L, bs)
    gla_s0 = jnp.pad(st_gla, ((0, 0), (0, 0), (0, GLA_DK_PAD - GLA_DK), (0, GLA_DV_PAD - GLA_DV)))
    o_gla, new_gla = _gla_call(p_gla, gla_s0, P['gla'], L, bs)
    conv0 = jnp.pad(st_conv, ((0, 0), (8 - (ML_CONV - 1), 0), (0, 0)))
    m0 = _pad_last(st_m, 128).reshape(B, 1, 128)
    o_ml, new_c, new_n, new_m = _mlstm_call(p_ml, conv0, st_c, st_n, m0, P['ml'], L, bs)
    x = _outproj_call(x, mods, o_rw, o_gla, o_ml, P['w_o_rw'], P['w_o_gla'], P['w_o_ml'],
                      P['ln_g'][1], P['ln_b'][1], bb, tt)
    x = _ffn_call(x, mods, 2, P['ffn_wg'][1], P['ffn_wu'][1], P['ffn_wd'][1], P['ln_g'][2], P['ln_b'][2], bb, tt)
    new = (new_rw, p_rw[:, -1, :], new_gla[:, :, :GLA_DK, :GLA_DV], new_c, new_n,
           new_m[:, 0, :ML_HEADS], p_ml[:, T - (ML_CONV - 1):, :2 * ML_W])
    return x, new


def _run_trunk(x, mods_all, states, layers, bb, tt, bs):
    B, T, _ = x.shape
    L = math.gcd(T, MIX_CHUNK)
    collected = [[] for _ in states]
    for l in range(DEPTH):
        x, new = _layer(x, mods_all[l], tuple(s[l] for s in states), layers[l], bb, tt, L, bs)
        for lst, s in zip(collected, new):
            lst.append(s)
    return x, tuple(jnp.stack(lst) for lst in collected)


def kernel(x_prompt, x_sample, c_prompt, c_sample, state_rwkv, state_rwkv_shift, state_gla, state_mlstm_c, state_mlstm_n, state_mlstm_m, state_mlstm_conv, ada_w, ada_b, ln_g, ln_b, ffn_wg, ffn_wu, ffn_wd, w_in, w_out, rw_mu, rw_w0, rw_w2, rw_a0, rw_a2, rw_g2, rw_k_k, rw_k_a, rw_r_k, rw_gn_g, rw_gn_b, gla_a2, gla_a_b, gla_gn_g, ml_conv_w, ml_conv_b, ml_i_b, ml_f_b, ml_gn_g):
    W = dict(ln_g=ln_g, ln_b=ln_b, ffn_wg=ffn_wg, ffn_wu=ffn_wu, ffn_wd=ffn_wd, w_in=w_in, w_out=w_out,
             rw_mu=rw_mu, rw_w0=rw_w0, rw_w2=rw_w2, rw_a0=rw_a0, rw_a2=rw_a2, rw_g2=rw_g2, rw_k_k=rw_k_k,
             rw_k_a=rw_k_a, rw_r_k=rw_r_k.reshape(DEPTH, RW_W), rw_gn_g=rw_gn_g, rw_gn_b=rw_gn_b,
             gla_a2=gla_a2, gla_a_b=gla_a_b, gla_gn_g=gla_gn_g, ml_conv_w=ml_conv_w, ml_conv_b=ml_conv_b,
             ml_i_b=ml_i_b, ml_f_b=ml_f_b, ml_gn_g=ml_gn_g)
    layers = [_prep_layer(l, W) for l in range(DEPTH)]
    Bp, Tp, _ = x_prompt.shape
    Bs, Ts, _ = x_sample.shape

    mods = _ada_call(jnp.concatenate([c_prompt, c_sample], axis=0), ada_w, ada_b)
    mods_p = mods[:, :Bp].reshape(DEPTH, Bp, 9, 1, D_MODEL)
    mods_s = mods[:, Bp:].reshape(DEPTH, Bs, 9, 1, D_MODEL)

    dt = x_prompt.dtype
    zero_states = (
        jnp.zeros((DEPTH, Bp, RW_HEADS, RW_HD, RW_HD), dt),
        jnp.zeros((DEPTH, Bp, RW_PROJ), dt),
        jnp.zeros((DEPTH, Bp, GLA_HEADS, GLA_DK, GLA_DV), dt),
        jnp.zeros((DEPTH, Bp, ML_HEADS, ML_HD, ML_HD), dt),
        jnp.zeros((DEPTH, Bp, ML_HEADS, ML_HD), dt),
        jnp.zeros((DEPTH, Bp, ML_HEADS), dt),
        jnp.zeros((DEPTH, Bp, ML_CONV - 1, 2 * ML_W), dt),
    )
    y_prompt, p_states = _run_trunk(x_prompt, mods_p, zero_states, layers, 1, 512, 1)
    sample_states = (state_rwkv, state_rwkv_shift, state_gla, state_mlstm_c, state_mlstm_n,
                     state_mlstm_m, state_mlstm_conv)
    y_sample, s_states = _run_trunk(x_sample, mods_s, sample_states, layers, 512 // Ts, Ts, 4)
    return (y_prompt, y_sample) + p_states + s_states
```

```python
import functools
import math

import jax
import jax.numpy as jnp
from jax import lax
from jax.experimental import pallas as pl
from jax.experimental.pallas import tpu as pltpu

F32 = jnp.float32
BF16 = jnp.bfloat16

D_MODEL = 1024
DEPTH = 4
D_FF = 2752
D_FF_PAD = 2816
FF_CHUNKS = 2

RW_HEADS, RW_HD = 6, 64
RW_W = RW_HEADS * RW_HD
RW_PROJ = 3 * RW_W + 64 + 64 + 128
RW_GN_EPS = 64e-5

GLA_HEADS, GLA_DK, GLA_DV = 4, 48, 96
GLA_DK_PAD, GLA_DV_PAD = 64, 128
GLA_QK_PAD = GLA_HEADS * GLA_DK_PAD
GLA_W_PAD = GLA_HEADS * GLA_DV_PAD
GLA_LORA = 16
GLA_PROJ_PAD = 2 * GLA_QK_PAD + 2 * GLA_W_PAD + 128
GLA_TAU = 16.0

ML_HEADS, ML_HD = 4, 64
ML_W = ML_HEADS * ML_HD
ML_CONV = 4
ML_PROJ_PAD = 4 * ML_W + 2 * 128

MIX_CHUNK = 64
ALPHA = (2 * DEPTH) ** 0.25
LN_EPS = 1e-5
NORM_EPS = 1e-6

VMEM_LIMIT = 56 * 1024 * 1024

_NN = (((1,), (0,)), ((), ()))
_NT = (((1,), (1,)), ((), ()))
_TN = (((0,), (0,)), ((), ()))


def _dg(a, b, dims=_NN):
    return lax.dot_general(a, b, dims, preferred_element_type=F32)


def _split2(x):
    hi = x.astype(BF16)
    lo = (x - hi.astype(F32)).astype(BF16)
    return hi, lo


def _split3(x):
    x1 = x.astype(BF16)
    r1 = x - x1.astype(F32)
    x2 = r1.astype(BF16)
    x3 = (r1 - x2.astype(F32)).astype(BF16)
    return x1, x2, x3


_sp = _split2


def _d3(a_s, b_s, dims=_NN):
    (ah, al), (bh, bl) = a_s, b_s
    return _dg(ah, bh, dims) + (_dg(ah, bl, dims) + _dg(al, bh, dims))


def _dot_mask_lhs(m_bf16, x, dims=_NN):
    x1, x2, x3 = _split3(x)
    return _dg(m_bf16, x1, dims) + (_dg(m_bf16, x2, dims) + _dg(m_bf16, x3, dims))


def _dot_mask_rhs(x, m_bf16):
    return _dg(x.astype(BF16), m_bf16)


def _sigmoid(x):
    return 1.0 / (1.0 + jnp.exp(-x))


def _log_sigmoid(x):
    return jnp.minimum(x, 0.0) - jnp.log1p(jnp.exp(-jnp.abs(x)))


def _tri_masks(n):
    row = lax.broadcasted_iota(jnp.int32, (n, n), 0)
    col = lax.broadcasted_iota(jnp.int32, (n, n), 1)
    return row >= col, row > col, row == col


def _modulate(x3, sc_ref, sh_ref):
    return x3 * (1.0 + sc_ref[:, 0]) + sh_ref[:, 0]


def _residual_ln(x3, upd3, gt_ref, lg_ref, lb_ref, w):
    y = ALPHA * x3 + (w * (1.0 + gt_ref[:, 0])) * upd3
    mu = jnp.mean(y, -1, keepdims=True)
    yc = y - mu
    var = jnp.mean(yc * yc, -1, keepdims=True)
    return yc * lax.rsqrt(var + LN_EPS) * lg_ref[...] + lb_ref[...]


def _ada_kernel(c_ref, w_ref, b_ref, o_ref):
    c = c_ref[...]
    s = (c * _sigmoid(c)).astype(BF16)
    o_ref[0] = _dg(s, w_ref[0].astype(BF16)) + b_ref[0]


def _ada_call(c_all, ada_w, ada_b):
    nb = c_all.shape[0]
    tn = 1536
    return pl.pallas_call(
        _ada_kernel,
        grid=(DEPTH, 9 * D_MODEL // tn),
        in_specs=[
            pl.BlockSpec((nb, D_MODEL), lambda l, j: (0, 0)),
            pl.BlockSpec((1, D_MODEL, tn), lambda l, j: (l, 0, j)),
            pl.BlockSpec((1, 1, tn), lambda l, j: (l, 0, j)),
        ],
        out_specs=pl.BlockSpec((1, nb, tn), lambda l, j: (l, 0, j)),
        out_shape=jax.ShapeDtypeStruct((DEPTH, nb, 9 * D_MODEL), F32),
        compiler_params=pltpu.CompilerParams(
            dimension_semantics=("parallel", "parallel"), vmem_limit_bytes=VMEM_LIMIT),
        name="ada_mods",
    )(c_all, ada_w, ada_b.reshape(DEPTH, 1, 9 * D_MODEL))


def _mod_spec(bb, k):
    return pl.BlockSpec((bb, 1, 1, D_MODEL), lambda i, j, k=k: (i, k, 0, 0))


def _const_spec(shape):
    nd = len(shape)
    return pl.BlockSpec(shape, lambda i, j, nd=nd: (0,) * nd, pipeline_mode=pl.Buffered(1))


def _ffn_kernel(x_ref, sh_ref, sc_ref, gt_ref, wg_ref, wu_ref, wd_ref, lg_ref, lb_ref, o_ref):
    bb, tt, d = x_ref.shape
    x3 = x_ref[...]
    h = _modulate(x3, sc_ref, sh_ref).reshape(bb * tt, d).astype(BF16)
    ck = D_FF_PAD // FF_CHUNKS
    acc = jnp.zeros((bb * tt, d), F32)
    for c in range(FF_CHUNKS):
        g = _dg(h, wg_ref[:, c * ck:(c + 1) * ck])
        u = _dg(h, wu_ref[:, c * ck:(c + 1) * ck])
        a = (g * _sigmoid(g) * u).astype(BF16)
        acc = acc + _dg(a, wd_ref[c * ck:(c + 1) * ck, :])
    o_ref[...] = _residual_ln(x3, acc.reshape(bb, tt, d), gt_ref, lg_ref, lb_ref, 0.5)


def _ffn_call(x, mods, sub, wg, wu, wd, lg, lb, bb, tt):
    B, T, d = x.shape
    return pl.pallas_call(
        _ffn_kernel,
        grid=(B // bb, T // tt),
        in_specs=[
            pl.BlockSpec((bb, tt, d), lambda i, j: (i, j, 0)),
            _mod_spec(bb, 3 * sub), _mod_spec(bb, 3 * sub + 1), _mod_spec(bb, 3 * sub + 2),
            _const_spec((d, D_FF_PAD)), _const_spec((d, D_FF_PAD)), _const_spec((D_FF_PAD, d)),
            _const_spec((1, d)), _const_spec((1, d)),
        ],
        out_specs=pl.BlockSpec((bb, tt, d), lambda i, j: (i, j, 0)),
        out_shape=jax.ShapeDtypeStruct((B, T, d), F32),
        compiler_params=pltpu.CompilerParams(
            dimension_semantics=("parallel", "parallel"), vmem_limit_bytes=VMEM_LIMIT),
        name="ffn_sublayer",
    )(x, mods, mods, mods, wg, wu, wd, lg, lb)


def _inproj_kernel(x_ref, sh_ref, sc_ref, w_ref, orw_ref, ogla_ref, oml_ref):
    bb, tt, d = x_ref.shape
    h = _modulate(x_ref[...], sc_ref, sh_ref).reshape(bb * tt, d).astype(BF16)
    o1, o2 = RW_PROJ, RW_PROJ + GLA_PROJ_PAD
    orw_ref[...] = _dg(h, w_ref[:, :o1]).reshape(bb, tt, RW_PROJ)
    ogla_ref[...] = _dg(h, w_ref[:, o1:o2]).reshape(bb, tt, GLA_PROJ_PAD)
    oml_ref[...] = _dg(h, w_ref[:, o2:]).reshape(bb, tt, ML_PROJ_PAD)


def _inproj_call(x, mods, w_in, bb, tt):
    B, T, d = x.shape
    widths = (RW_PROJ, GLA_PROJ_PAD, ML_PROJ_PAD)
    return pl.pallas_call(
        _inproj_kernel,
        grid=(B // bb, T // tt),
        in_specs=[
            pl.BlockSpec((bb, tt, d), lambda i, j: (i, j, 0)),
            _mod_spec(bb, 3), _mod_spec(bb, 4),
            _const_spec((d, sum(widths))),
        ],
        out_specs=[pl.BlockSpec((bb, tt, w), lambda i, j: (i, j, 0)) for w in widths],
        out_shape=[jax.ShapeDtypeStruct((B, T, w), F32) for w in widths],
        compiler_params=pltpu.CompilerParams(
            dimension_semantics=("parallel", "parallel"), vmem_limit_bytes=VMEM_LIMIT),
        name="mixer_in_proj",
    )(x, mods, mods, w_in)


def _outproj_kernel(x_ref, gt_ref, orw_ref, ogla_ref, oml_ref, w1_ref, w2_ref, w3_ref,
                    lg_ref, lb_ref, o_ref):
    bb, tt, d = x_ref.shape
    flat = lambda ref: ref[...].reshape(bb * tt, ref.shape[-1]).astype(BF16)
    mix = _dg(flat(orw_ref), w1_ref[...]) + _dg(flat(ogla_ref), w2_ref[...]) + _dg(flat(oml_ref), w3_ref[...])
    o_ref[...] = _residual_ln(x_ref[...], mix.reshape(bb, tt, d), gt_ref, lg_ref, lb_ref, 1.0)


def _outproj_call(x, mods, o_rw, o_gla, o_ml, w1, w2, w3, lg, lb, bb, tt):
    B, T, d = x.shape
    tok = lambda w: pl.BlockSpec((bb, tt, w), lambda i, j: (i, j, 0))
    return pl.pallas_call(
        _outproj_kernel,
        grid=(B // bb, T // tt),
        in_specs=[
            tok(d), _mod_spec(bb, 5), tok(RW_W), tok(GLA_W_PAD), tok(ML_W),
            _const_spec((RW_W, d)), _const_spec((GLA_W_PAD, d)), _const_spec((ML_W, d)),
            _const_spec((1, d)), _const_spec((1, d)),
        ],
        out_specs=tok(d),
        out_shape=jax.ShapeDtypeStruct((B, T, d), F32),
        compiler_params=pltpu.CompilerParams(
            dimension_semantics=("parallel", "parallel"), vmem_limit_bytes=VMEM_LIMIT),
        name="mixer_out_proj",
    )(x, mods, o_rw, o_gla, o_ml, w1, w2, w3, lg, lb)


def _block_tri(n, L):
    row = lax.broadcasted_iota(jnp.int32, (n, n), 0)
    col = lax.broadcasted_iota(jnp.int32, (n, n), 1)
    incl = row >= col
    if n > L:
        sh = int(math.log2(L))
        incl = incl & (lax.shift_right_logical(row, sh) == lax.shift_right_logical(col, sh))
    return incl


def _rows3(x, bs, L):
    return x.reshape(bs, L, x.shape[-1])


def _lane_half(shape):
    return (lax.broadcasted_iota(jnp.int32, shape, len(shape) - 1) & 64) != 0


def _head_dups(x, heads):
    outs = []
    for j in range(heads // 2):
        blk = x[:, 128 * j:128 * (j + 1)]
        swapped = pltpu.roll(blk, 64, 1)
        upper = _lane_half(blk.shape)
        outs.append(jnp.where(upper, swapped, blk))
        outs.append(jnp.where(upper, blk, swapped))
    return outs


def _hl(x):
    hi = x.astype(BF16)
    return hi, (x - hi.astype(F32)).astype(BF16)


def _rows_cat(pieces):
    if pieces[0].shape[0] % 16 == 0:
        return jnp.concatenate(pieces, axis=0)
    return jnp.concatenate([t.astype(F32) for t in pieces], axis=0).astype(BF16)


def _kc_lhs(hl):
    hi, lo = hl
    v0 = jnp.where(_lane_half(hi.shape), lo, hi)
    return jnp.concatenate([v0, v0], axis=1)


def _kc_rhs_t(hl):
    hi, lo = hl
    return jnp.concatenate([hi, lo], axis=1)


def _kc_rhs(hl):
    hi, lo = hl
    return _rows_cat([hi, hi, lo, lo])


def _kc_lhs_t(hl):
    hi, lo = hl
    return _rows_cat([hi, lo, hi, lo])


def _rwkv_kernel(p_ref, sh0_ref, s0_ref, mu_ref, w0_ref, a0_ref, wl_ref, g2_ref, kk_ref, ka_ref,
                 rk_ref, gg_ref, gb_ref, seg_ref, o_ref, sT_ref, s_scr, prev_scr):
    bs, L, _ = p_ref.shape
    n = bs * L
    G = RW_HD
    ng = n // G
    spg = G // L
    ci = pl.program_id(1)

    @pl.when(ci == 0)
    def _():
        s0 = s0_ref[...]
        s_scr[...] = jnp.concatenate([s0, s0], axis=-1)
        prev_scr[...] = sh0_ref[...]

    p3 = p_ref[...]
    rolled = _rows3(pltpu.roll(p3.reshape(n, RW_PROJ), 1, 0), bs, L)
    row = lax.broadcasted_iota(jnp.int32, (1, L, 1), 1)
    prev = jnp.where(row == 0, prev_scr[...], rolled)
    prev_scr[...] = p3[:, L - 1:L, :]
    xs = (p3 + (prev - p3) * mu_ref[...]).reshape(n, RW_PROJ)

    r = xs[:, 0:RW_W]
    k = xs[:, RW_W:2 * RW_W]
    v = xs[:, 2 * RW_W:3 * RW_W]
    lin = xs[:, 3 * RW_W:3 * RW_W + 128]
    lane = lax.broadcasted_iota(jnp.int32, (1, 128), 1)
    lin = jnp.where(lane < 64, jnp.tanh(lin), lin)
    dl = _dg(lin.astype(BF16), wl_ref[...])
    lw = -jnp.exp(_log_sigmoid(w0_ref[...] + dl[:, :RW_W]) - 0.5)
    a_sig = _sigmoid(a0_ref[...] + dl[:, RW_W:])
    g = _dg(_sigmoid(xs[:, 3 * RW_W + 128:]).astype(BF16), g2_ref[...])

    seg = seg_ref[...]
    kk = k * kk_ref[...]
    kk = kk * lax.rsqrt(jnp.maximum(_dot_mask_rhs(kk * kk, seg), 1e-24))
    k2 = k * (1.0 + (a_sig - 1.0) * ka_ref[...])
    bv = kk * a_sig

    sh = int(math.log2(L))
    trow = lax.broadcasted_iota(jnp.int32, (G, 128), 0)
    tcol = lax.broadcasted_iota(jnp.int32, (G, 128), 1) & (G - 1)
    same = lax.shift_right_logical(trow, sh) == lax.shift_right_logical(tcol, sh)
    incl = (trow >= tcol) & same
    strict = (trow > tcol) & same
    eye = trow == tcol

    crow = lax.broadcasted_iota(jnp.int32, (G, 3 * G), 0)
    ccol = lax.broadcasted_iota(jnp.int32, (G, 3 * G), 1) & (G - 1)
    tri3 = ((crow >= ccol)
            & (lax.shift_right_logical(crow, sh) == lax.shift_right_logical(ccol, sh))).astype(BF16)
    grp = lambda t, gi: t[gi * G:(gi + 1) * G]
    c = jnp.concatenate(
        [_dg(tri3, jnp.concatenate(_split3(grp(lw, gi)), axis=0)) for gi in range(ng)], axis=0)
    c3 = _rows3(c, bs, L)
    cl3 = c3[:, L - 1:L, :]
    einv = jnp.exp(-c)
    edl = jnp.exp(cl3 - c3).reshape(n, RW_W)
    a_t = -kk * jnp.exp(c - lw)
    r_t = r * jnp.exp(c)
    p_end = jnp.exp(cl3)

    H = range(RW_HEADS)
    units = [(gi, h) for gi in range(ng) for h in H]
    U = lambda f: [f(gi, h) for gi, h in units]
    idx = range(len(units))
    at_d, rt_d = _head_dups(a_t, RW_HEADS), _head_dups(r_t, RW_HEADS)
    kh_d, bh_d = _head_dups(k2 * einv, RW_HEADS), _head_dups(bv * einv, RW_HEADS)
    ke_d, be_d = _head_dups(k2 * edl, RW_HEADS), _head_dups(bv * edl, RW_HEADS)
    v_d = _head_dups(v, RW_HEADS)
    pe_d = [_head_dups(jnp.broadcast_to(p_end[b], (8, RW_W)), RW_HEADS) for b in range(bs)]

    at_c = U(lambda gi, h: _kc_lhs(_hl(grp(at_d[h], gi))))
    rt_c = U(lambda gi, h: _kc_lhs(_hl(grp(rt_d[h], gi))))
    ar_c = [jnp.concatenate([at_c[i], rt_c[i]], axis=0) for i in idx]
    twice = lambda t: jnp.concatenate([t, t], axis=0)
    gk = U(lambda gi, h: twice(_kc_rhs_t(_hl(grp(kh_d[h], gi)))))
    gb = U(lambda gi, h: twice(_kc_rhs_t(_hl(grp(bh_d[h], gi)))))
    gk = [_dg(ar_c[i], gk[i], _NT) for i in idx]
    gb = [_dg(ar_c[i], gb[i], _NT) for i in idx]
    a_ak = [_kc_lhs(_hl(jnp.where(strict, t[:G], 0.0))) for t in gk]
    a_rk = [_kc_lhs(_hl(jnp.where(incl, t[G:], 0.0))) for t in gk]
    a_ab = [jnp.where(strict, t[:G], 0.0) for t in gb]
    a_rb = [_kc_lhs(_hl(jnp.where(incl, t[G:], 0.0))) for t in gb]
    t_inv = [jnp.where(eye, 1.0, t) for t in a_ab]
    pw = a_ab
    for _ in range(sh - 1):
        pw_hl = [_hl(t) for t in pw]
        pw = [_dg(_kc_lhs(t), _kc_rhs(t)) for t in pw_hl]
        pw_r = [_kc_rhs(_hl(t)) for t in pw]
        t_inv = [t_inv[i] + _dg(_kc_lhs(_hl(t_inv[i])), pw_r[i]) for i in idx]

    seqs = lambda gi: range(gi * spg, (gi + 1) * spg)
    srow = lambda t, b: t[b * L:(b + 1) * L]
    s0 = {(b, h): s_scr[b, h] for b in range(bs) for h in H}
    s0_t = {key: twice(_kc_rhs_t(_hl(val))) for key, val in s0.items()}
    if spg == 1:
        x_st = U(lambda gi, h: _dg(at_c[gi * RW_HEADS + h], s0_t[gi, h], _NT))
        y_st = U(lambda gi, h: _dg(rt_c[gi * RW_HEADS + h], s0_t[gi, h], _NT))
    else:
        per_seq = lambda t_d, gi, h: jnp.concatenate(
            [_dg(_kc_lhs(_hl(srow(t_d[h], b))), s0_t[b, h], _NT) for b in seqs(gi)], axis=0)
        x_st = U(lambda gi, h: per_seq(at_d, gi, h))
        y_st = U(lambda gi, h: per_seq(rt_d, gi, h))
    v_r = U(lambda gi, h: _kc_rhs(_hl(grp(v_d[h], gi))))
    x = [x_st[i] + _dg(a_ak[i], v_r[i]) for i in idx]
    u = [_dg(_kc_lhs(_hl(t_inv[i])), _kc_rhs(_hl(x[i]))) for i in idx]
    u_r = [_kc_rhs(_hl(t)) for t in u]
    ys = [y_st[i] + _dg(a_rk[i], v_r[i]) + _dg(a_rb[i], u_r[i]) for i in idx]
    for i, (gi, h) in enumerate(units):
        for j, b in enumerate(seqs(gi)):
            vt = _kc_lhs_t(_hl(srow(v_d[h], b)[:, :RW_HD]))
            ut = _kc_lhs_t(_hl(u[i][j * L:(j + 1) * L, :RW_HD]))
            s_scr[b, h] = (s0[b, h] * pe_d[b][h][0:1, :]
                           + _dg(vt, _kc_rhs(_hl(srow(ke_d[h], b))), _TN)
                           + _dg(ut, _kc_rhs(_hl(srow(be_d[h], b))), _TN))
    upper = _lane_half((G, 128))
    y = jnp.concatenate(
        [jnp.concatenate([jnp.where(upper, ys[gi * RW_HEADS + 2 * j + 1], ys[gi * RW_HEADS + 2 * j])
                          for j in range(RW_HEADS // 2)], axis=-1) for gi in range(ng)], axis=0)

    inv_hd = 1.0 / RW_HD
    yc = y - _dot_mask_rhs(y, seg) * inv_hd
    var = _dot_mask_rhs(yc * yc, seg) * inv_hd
    yn = yc * lax.rsqrt(var + RW_GN_EPS) * gg_ref[...] + gb_ref[...]
    bonus = _dot_mask_rhs(r * k2 * rk_ref[...], seg) * v
    o_ref[...] = _rows3((yn + bonus) * g, bs, L)

    @pl.when(ci == pl.num_programs(1) - 1)
    def _():
        sT_ref[...] = s_scr[...][..., :RW_HD]


def _seq_spec(bs, shape):
    nd = len(shape)
    return pl.BlockSpec((bs,) + shape, lambda b, c, nd=nd: (b,) + (0,) * nd)


def _tok_spec(bs, L, w):
    return pl.BlockSpec((bs, L, w), lambda b, c: (b, c, 0))


def _param_spec(shape):
    nd = len(shape)
    return pl.BlockSpec(shape, lambda b, c, nd=nd: (0,) * nd)


def _mixer_params():
    return pltpu.CompilerParams(dimension_semantics=("parallel", "arbitrary"), vmem_limit_bytes=VMEM_LIMIT)


def _rwkv_call(p, shift0, s0, prm, L, bs):
    B, T, _ = p.shape
    mu, w0, a0, wl, g2, k_k, k_a, r_k, gn_g, gn_b, seg = prm
    row = lambda w: _param_spec((1, w))
    st = (RW_HEADS, RW_HD, RW_HD)
    return pl.pallas_call(
        _rwkv_kernel,
        grid=(B // bs, T // L),
        in_specs=[
            _tok_spec(bs, L, RW_PROJ), _seq_spec(bs, (1, RW_PROJ)), _seq_spec(bs, st),
            row(RW_PROJ), row(RW_W), row(RW_W), _param_spec((128, 2 * RW_W)), _param_spec((128, RW_W)),
            row(RW_W), row(RW_W), row(RW_W), row(RW_W), row(RW_W), _param_spec((RW_W, RW_W)),
        ],
        out_specs=[_tok_spec(bs, L, RW_W), _seq_spec(bs, st)],
        out_shape=[jax.ShapeDtypeStruct((B, T, RW_W), F32), jax.ShapeDtypeStruct((B,) + st, F32)],
        scratch_shapes=[pltpu.VMEM((bs, RW_HEADS, RW_HD, 2 * RW_HD), F32), pltpu.VMEM((bs, 1, RW_PROJ), F32)],
        compiler_params=_mixer_params(),
        name="rwkv7_mixer",
    )(p, shift0.reshape(B, 1, RW_PROJ), s0, mu, w0, a0, wl, g2, k_k, k_a, r_k, gn_g, gn_b, seg)


def _gla_kernel(p_ref, s0_ref, a2_ref, ab_ref, gg_ref, o_ref, sT_ref, s_scr):
    bs, L, _ = p_ref.shape
    n = bs * L
    ci = pl.program_id(1)

    @pl.when(ci == 0)
    def _():
        s_scr[...] = s0_ref[...]

    p = p_ref[...].reshape(n, GLA_PROJ_PAD)
    q = p[:, 0:GLA_QK_PAD]
    k = p[:, GLA_QK_PAD:2 * GLA_QK_PAD]
    v = p[:, 2 * GLA_QK_PAD:2 * GLA_QK_PAD + GLA_W_PAD]
    gr = p[:, 2 * GLA_QK_PAD + GLA_W_PAD:2 * GLA_QK_PAD + 2 * GLA_W_PAD]
    gd = p[:, 2 * GLA_QK_PAD + 2 * GLA_W_PAD:]
    log_a = _log_sigmoid(_dg(gd.astype(BF16), a2_ref[...]) + ab_ref[...]) * (1.0 / GLA_TAU)

    bc = _dot_mask_lhs(_block_tri(n, L).astype(BF16), log_a)
    bc3 = _rows3(bc, bs, L)
    bl3 = bc3[:, L - 1:L, :]
    qd = q * (GLA_DK ** -0.5) * jnp.exp(bc)
    kd = k * jnp.exp(-bc)
    ks = k * jnp.exp(bl3 - bc3).reshape(n, GLA_QK_PAD)
    d_end = jnp.exp(bl3)

    incl, _, _ = _tri_masks(L)
    _, _, eye_k = _tri_masks(GLA_DK_PAD)
    units = [(b, h) for b in range(bs) for h in range(GLA_HEADS)]
    idx = range(len(units))
    kblk = lambda t, b, h: t[b * L:(b + 1) * L, h * GLA_DK_PAD:(h + 1) * GLA_DK_PAD]
    vblk = lambda t, b, h: t[b * L:(b + 1) * L, h * GLA_DV_PAD:(h + 1) * GLA_DV_PAD]
    U = lambda f: [f(b, h) for b, h in units]

    qd_s = U(lambda b, h: _sp(kblk(qd, b, h)))
    kd_s = U(lambda b, h: _sp(kblk(kd, b, h)))
    ks_s = U(lambda b, h: _sp(kblk(ks, b, h)))
    v_s = U(lambda b, h: _sp(vblk(v, b, h)))
    a_s = [_sp(jnp.where(incl, _d3(qd_s[i], kd_s[i], _NT), 0.0)) for i in idx]
    s0_s = U(lambda b, h: _sp(s_scr[b, h]))
    o = [_d3(a_s[i], v_s[i]) + _d3(qd_s[i], s0_s[i]) for i in idx]
    dm_s = U(lambda b, h: _sp(jnp.where(eye_k, jnp.broadcast_to(
        d_end[b, :, h * GLA_DK_PAD:(h + 1) * GLA_DK_PAD], (GLA_DK_PAD, GLA_DK_PAD)), 0.0)))
    for i, (b, h) in enumerate(units):
        s_scr[b, h] = _d3(dm_s[i], s0_s[i]) + _d3(ks_s[i], v_s[i], _TN)
    outs = []
    for i, (b, h) in enumerate(units):
        ms = jnp.sum(o[i] * o[i], -1, keepdims=True) * (1.0 / GLA_DV)
        sv = slice(h * GLA_DV_PAD, (h + 1) * GLA_DV_PAD)
        grh = vblk(gr, b, h)
        outs.append(o[i] * lax.rsqrt(ms + NORM_EPS) * gg_ref[:, sv] * (grh * _sigmoid(grh)))
    out = jnp.concatenate(
        [jnp.concatenate(outs[b * GLA_HEADS:(b + 1) * GLA_HEADS], axis=-1) for b in range(bs)], axis=0)
    o_ref[...] = _rows3(out, bs, L)

    @pl.when(ci == pl.num_programs(1) - 1)
    def _():
        sT_ref[...] = s_scr[...]


def _gla_call(p, s0, prm, L, bs):
    B, T, _ = p.shape
    a2, a_b, gn_g = prm
    st = (GLA_HEADS, GLA_DK_PAD, GLA_DV_PAD)
    return pl.pallas_call(
        _gla_kernel,
        grid=(B // bs, T // L),
        in_specs=[
            _tok_spec(bs, L, GLA_PROJ_PAD), _seq_spec(bs, st),
            _param_spec((128, GLA_QK_PAD)), _param_spec((1, GLA_QK_PAD)), _param_spec((1, GLA_W_PAD)),
        ],
        out_specs=[_tok_spec(bs, L, GLA_W_PAD), _seq_spec(bs, st)],
        out_shape=[jax.ShapeDtypeStruct((B, T, GLA_W_PAD), F32), jax.ShapeDtypeStruct((B,) + st, F32)],
        scratch_shapes=[pltpu.VMEM((bs,) + st, F32)],
        compiler_params=_mixer_params(),
        name="gla_mixer",
    )(p, s0, a2, a_b, gn_g)


def _mlstm_kernel(p_ref, cv0_ref, c0_ref, n0_ref, m0_ref, cw_ref, cb_ref, ib_ref, fb_ref, gg_ref,
                  o_ref, cT_ref, nT_ref, mT_ref, c_scr, n_scr, m_scr, cv_scr):
    bs, L, _ = p_ref.shape
    n = bs * L
    ci = pl.program_id(1)

    @pl.when(ci == 0)
    def _():
        c_scr[...] = c0_ref[...]
        n_scr[...] = n0_ref[...]
        m_scr[...] = m0_ref[...]
        cv_scr[...] = cv0_ref[...]

    p = p_ref[...].reshape(n, ML_PROJ_PAD)
    x = p[:, 0:2 * ML_W]
    row = lax.broadcasted_iota(jnp.int32, (1, L, 1), 1)
    conv = cb_ref[...] + x * cw_ref[ML_CONV - 1:ML_CONV, :]
    for s in range(1, ML_CONV):
        cr = jnp.concatenate([pltpu.roll(cv_scr[b], s, 0) for b in range(bs) for _ in range(L // 8)], axis=0)
        shifted = jnp.where(row < s, _rows3(cr, bs, L), _rows3(pltpu.roll(x, s, 0), bs, L))
        conv = conv + shifted.reshape(n, 2 * ML_W) * cw_ref[ML_CONV - 1 - s:ML_CONV - s, :]
    cv_scr[...] = _rows3(x, bs, L)[:, L - 8:L, :]
    qk = conv * _sigmoid(conv)
    q = qk[:, :ML_W]
    k = qk[:, ML_W:] * (ML_HD ** -0.5)
    v = p[:, 2 * ML_W:3 * ML_W]
    o_raw = p[:, 3 * ML_W:4 * ML_W]
    gi = p[:, 4 * ML_W:4 * ML_W + 128] + ib_ref[...]
    lf = _log_sigmoid(p[:, 4 * ML_W + 128:] + fb_ref[...])

    bcum = _dot_mask_lhs(_block_tri(n, L).astype(BF16), lf)
    b3 = _rows3(bcum, bs, L)
    bl3 = b3[:, L - 1:L, :]
    m_prev3 = m_scr[...]
    inter = (b3 + m_prev3).reshape(n, 128)
    d = gi - bcum
    lane = lax.broadcasted_iota(jnp.int32, (1, 128), 1)
    ones_l = jnp.ones((L, 128), BF16)
    incl, _, _ = _tri_masks(L)

    units = [(b, h) for b in range(bs) for h in range(ML_HEADS)]
    idx = range(len(units))
    U = lambda f: [f(b, h) for b, h in units]
    rows = lambda t, b: t[b * L:(b + 1) * L]
    blk = lambda t, b, h: t[b * L:(b + 1) * L, h * ML_HD:(h + 1) * ML_HD]
    pick = lambda t, h: jnp.sum(jnp.where(lane == h, t, 0.0), -1, keepdims=True)

    b_col = U(lambda b, h: pick(rows(bcum, b), h))
    gi_col = U(lambda b, h: pick(rows(gi, b), h))
    inter_col = U(lambda b, h: pick(rows(inter, b), h))
    bl_h = U(lambda b, h: pick(bl3[b], h))
    mp_h = U(lambda b, h: pick(m_prev3[b], h))
    d_row = U(lambda b, h: _dot_mask_lhs(ones_l, jnp.where(lane == h, rows(d, b), 0.0), _NT))
    dmat = [jnp.where(incl, b_col[i] + d_row[i], -jnp.inf) for i in idx]
    m_t = [jnp.maximum(inter_col[i], jnp.max(dmat[i], -1, keepdims=True)) for i in idx]
    dw = [jnp.exp(dmat[i] - m_t[i]) for i in idx]
    w_int = [jnp.exp(inter_col[i] - m_t[i]) for i in idx]
    q_s = U(lambda b, h: _sp(blk(q, b, h)))
    k_s = U(lambda b, h: _sp(blk(k, b, h)))
    v_s = U(lambda b, h: _sp(blk(v, b, h)))
    s = [_d3(q_s[i], k_s[i], _NT) * dw[i] for i in idx]
    cmat = U(lambda b, h: c_scr[b, h])
    n_row = U(lambda b, h: n_scr[b, h:h + 1, :])
    num = [_d3(_sp(s[i]), v_s[i]) + w_int[i] * _d3(q_s[i], _sp(cmat[i]), _NT) for i in idx]
    den = [jnp.sum(s[i], -1, keepdims=True)
           + w_int[i] * jnp.sum(blk(q, *units[i]) * n_row[i], -1, keepdims=True) for i in idx]
    hc = [num[i] / jnp.maximum(jnp.abs(den[i]), jnp.exp(-m_t[i])) for i in idx]
    m_last = [t[L - 1:L, :] for t in m_t]
    w_state = [jnp.exp(bl_h[i] + mp_h[i] - m_last[i]) for i in idx]
    w_j = [jnp.exp(bl_h[i] - b_col[i] + gi_col[i] - m_last[i]) for i in idx]
    for i, (b, h) in enumerate(units):
        kh, vh = blk(k, b, h), blk(v, b, h)
        c_scr[b, h] = cmat[i] * w_state[i] + _d3(_sp(w_j[i] * vh), k_s[i], _TN)
        n_scr[b, h:h + 1, :] = n_row[i] * w_state[i] + jnp.sum(w_j[i] * kh, 0, keepdims=True)
    for b in range(bs):
        m_new = m_prev3[b]
        for h in range(ML_HEADS):
            m_new = jnp.where(lane == h, m_last[b * ML_HEADS + h], m_new)
        m_scr[b] = m_new
    outs = []
    for i, (b, h) in enumerate(units):
        mu = jnp.mean(hc[i], -1, keepdims=True)
        hcc = hc[i] - mu
        var = jnp.mean(hcc * hcc, -1, keepdims=True)
        outs.append(hcc * lax.rsqrt(var + NORM_EPS) * gg_ref[:, h * ML_HD:(h + 1) * ML_HD])
    out = jnp.concatenate(
        [jnp.concatenate(outs[b * ML_HEADS:(b + 1) * ML_HEADS], axis=-1) for b in range(bs)], axis=0)
    o_ref[...] = _rows3(out * _sigmoid(o_raw), bs, L)

    @pl.when(ci == pl.num_programs(1) - 1)
    def _():
        cT_ref[...] = c_scr[...]
        nT_ref[...] = n_scr[...]
        mT_ref[...] = m_scr[...]


def _mlstm_call(p, conv0, c0, n0, m0, prm, L, bs):
    B, T, _ = p.shape
    cw, cb, ib, fb, gn_g = prm
    st_c, st_n, st_m, st_cv = (ML_HEADS, ML_HD, ML_HD), (ML_HEADS, ML_HD), (1, 128), (8, 2 * ML_W)
    return pl.pallas_call(
        _mlstm_kernel,
        grid=(B // bs, T // L),
        in_specs=[
            _tok_spec(bs, L, ML_PROJ_PAD),
            _seq_spec(bs, st_cv), _seq_spec(bs, st_c), _seq_spec(bs, st_n), _seq_spec(bs, st_m),
            _param_spec((ML_CONV, 2 * ML_W)), _param_spec((1, 2 * ML_W)), _param_spec((1, 128)),
            _param_spec((1, 128)), _param_spec((1, ML_W)),
        ],
        out_specs=[_tok_spec(bs, L, ML_W), _seq_spec(bs, st_c), _seq_spec(bs, st_n), _seq_spec(bs, st_m)],
        out_shape=[jax.ShapeDtypeStruct((B, T, ML_W), F32),
                   jax.ShapeDtypeStruct((B,) + st_c, F32),
                   jax.ShapeDtypeStruct((B,) + st_n, F32),
                   jax.ShapeDtypeStruct((B,) + st_m, F32)],
        scratch_shapes=[pltpu.VMEM((bs,) + st_c, F32), pltpu.VMEM((bs,) + st_n, F32),
                        pltpu.VMEM((bs,) + st_m, F32), pltpu.VMEM((bs,) + st_cv, F32)],
        compiler_params=_mixer_params(),
        name="mlstm_mixer",
    )(p, conv0, c0, n0, m0, cw, cb, ib, fb, gn_g)


def _pad_heads(w, heads, width, padded):
    lead = w.shape[:-1]
    w = w.reshape(lead + (heads, width))
    w = jnp.pad(w, [(0, 0)] * len(lead) + [(0, 0), (0, padded - width)])
    return w.reshape(lead + (heads * padded,))


def _pad_last(w, n):
    return jnp.pad(w, [(0, 0)] * (w.ndim - 1) + [(0, n - w.shape[-1])])


def _prep_layer(l, W):
    gla_qk, gla_w = GLA_HEADS * GLA_DK, GLA_HEADS * GLA_DV
    w_in = W['w_in'][l]
    w_rw = w_in[:, :RW_PROJ]
    g = w_in[:, RW_PROJ:RW_PROJ + 2 * gla_qk + 2 * gla_w + GLA_LORA]
    m = w_in[:, RW_PROJ + 2 * gla_qk + 2 * gla_w + GLA_LORA:]
    g_q, g_k = g[:, :gla_qk], g[:, gla_qk:2 * gla_qk]
    g_v = g[:, 2 * gla_qk:2 * gla_qk + gla_w]
    g_gd = g[:, 2 * gla_qk + gla_w:2 * gla_qk + gla_w + GLA_LORA]
    g_gr = g[:, 2 * gla_qk + gla_w + GLA_LORA:]
    w_gla = jnp.concatenate([
        _pad_heads(g_q, GLA_HEADS, GLA_DK, GLA_DK_PAD), _pad_heads(g_k, GLA_HEADS, GLA_DK, GLA_DK_PAD),
        _pad_heads(g_v, GLA_HEADS, GLA_DV, GLA_DV_PAD), _pad_heads(g_gr, GLA_HEADS, GLA_DV, GLA_DV_PAD),
        _pad_last(g_gd, 128)], axis=1)
    w_ml = jnp.concatenate([m[:, :4 * ML_W], _pad_last(m[:, 4 * ML_W:4 * ML_W + ML_HEADS], 128),
                            _pad_last(m[:, 4 * ML_W + ML_HEADS:], 128)], axis=1)
    w_out = W['w_out'][l]
    w_o_gla = w_out[RW_W:RW_W + gla_w].reshape(GLA_HEADS, GLA_DV, D_MODEL)
    w_o_gla = jnp.pad(w_o_gla, ((0, 0), (0, GLA_DV_PAD - GLA_DV), (0, 0))).reshape(GLA_W_PAD, D_MODEL)

    zeros64 = jnp.zeros((64, RW_W), F32)
    w_lora = jnp.concatenate([jnp.concatenate([W['rw_w2'][l], zeros64], 1),
                              jnp.concatenate([zeros64, W['rw_a2'][l]], 1)], 0)
    a2 = jnp.pad(_pad_heads(W['gla_a2'][l], GLA_HEADS, GLA_DK, GLA_DK_PAD), ((0, 128 - GLA_LORA), (0, 0)))
    row = lambda t: t.reshape(1, -1)
    return dict(
        ffn_wg=[_pad_last(W['ffn_wg'][l, i], D_FF_PAD).astype(BF16) for i in range(2)],
        ffn_wu=[_pad_last(W['ffn_wu'][l, i], D_FF_PAD).astype(BF16) for i in range(2)],
        ffn_wd=[jnp.pad(W['ffn_wd'][l, i], ((0, D_FF_PAD - D_FF), (0, 0))).astype(BF16) for i in range(2)],
        ln_g=[row(W['ln_g'][l, i]) for i in range(3)],
        ln_b=[row(W['ln_b'][l, i]) for i in range(3)],
        w_in=jnp.concatenate([w_rw, w_gla, w_ml], axis=1).astype(BF16),
        w_o_rw=w_out[:RW_W].astype(BF16), w_o_gla=w_o_gla.astype(BF16),
        w_o_ml=w_out[RW_W + gla_w:].astype(BF16),
        rw=(row(W['rw_mu'][l]), row(W['rw_w0'][l]), row(W['rw_a0'][l]), w_lora.astype(BF16),
            W['rw_g2'][l].astype(BF16), row(W['rw_k_k'][l]), row(W['rw_k_a'][l]), row(W['rw_r_k'][l]),
            row(W['rw_gn_g'][l]), row(W['rw_gn_b'][l]),
            jnp.kron(jnp.eye(RW_HEADS, dtype=F32), jnp.ones((RW_HD, RW_HD), F32)).astype(BF16)),
        gla=(a2.astype(BF16), row(_pad_heads(W['gla_a_b'][l], GLA_HEADS, GLA_DK, GLA_DK_PAD)),
             row(_pad_heads(W['gla_gn_g'][l], GLA_HEADS, GLA_DV, GLA_DV_PAD))),
        ml=(W['ml_conv_w'][l], row(W['ml_conv_b'][l]), row(_pad_last(W['ml_i_b'][l], 128)),
            row(_pad_last(W['ml_f_b'][l], 128)), row(W['ml_gn_g'][l])),
    )


def _layer(x, mods, st, P, bb, tt, L, bs):
    B, T, _ = x.shape
    st_rw, st_shift, st_gla, st_c, st_n, st_m, st_conv = st
    x = _ffn_call(x, mods, 0, P['ffn_wg'][0], P['ffn_wu'][0], P['ffn_wd'][0], P['ln_g'][0], P['ln_b'][0], bb, tt)
    p_rw, p_gla, p_ml = _inproj_call(x, mods, P['w_in'], bb, tt)
    o_rw, new_rw = _rwkv_call(p_rw, st_shift, st_rw, P['rw'], L, bs[0])
    gla_s0 = jnp.pad(st_gla, ((0, 0), (0, 0), (0, GLA_DK_PAD - GLA_DK), (0, GLA_DV_PAD - GLA_DV)))
    o_gla, new_gla = _gla_call(p_gla, gla_s0, P['gla'], L, bs[1])
    conv0 = jnp.pad(st_conv, ((0, 0), (8 - (ML_CONV - 1), 0), (0, 0)))
    m0 = _pad_last(st_m, 128).reshape(B, 1, 128)
    o_ml, new_c, new_n, new_m = _mlstm_call(p_ml, conv0, st_c, st_n, m0, P['ml'], L, bs[2])
    x = _outproj_call(x, mods, o_rw, o_gla, o_ml, P['w_o_rw'], P['w_o_gla'], P['w_o_ml'],
                      P['ln_g'][1], P['ln_b'][1], bb, tt)
    x = _ffn_call(x, mods, 2, P['ffn_wg'][1], P['ffn_wu'][1], P['ffn_wd'][1], P['ln_g'][2], P['ln_b'][2], bb, tt)
    new = (new_rw, p_rw[:, -1, :], new_gla[:, :, :GLA_DK, :GLA_DV], new_c, new_n,
           new_m[:, 0, :ML_HEADS], p_ml[:, T - (ML_CONV - 1):, :2 * ML_W])
    return x, new


def _run_trunk(x, mods_all, states, layers, bb, tt, bs):
    B, T, _ = x.shape
    L = math.gcd(T, MIX_CHUNK)
    collected = [[] for _ in states]
    for l in range(DEPTH):
        x, new = _layer(x, mods_all[l], tuple(s[l] for s in states), layers[l], bb, tt, L, bs)
        for lst, s in zip(collected, new):
            lst.append(s)
    return x, tuple(jnp.stack(lst) for lst in collected)


def kernel(x_prompt, x_sample, c_prompt, c_sample, state_rwkv, state_rwkv_shift, state_gla, state_mlstm_c, state_mlstm_n, state_mlstm_m, state_mlstm_conv, ada_w, ada_b, ln_g, ln_b, ffn_wg, ffn_wu, ffn_wd, w_in, w_out, rw_mu, rw_w0, rw_w2, rw_a0, rw_a2, rw_g2, rw_k_k, rw_k_a, rw_r_k, rw_gn_g, rw_gn_b, gla_a2, gla_a_b, gla_gn_g, ml_conv_w, ml_conv_b, ml_i_b, ml_f_b, ml_gn_g):
    W = dict(ln_g=ln_g, ln_b=ln_b, ffn_wg=ffn_wg, ffn_wu=ffn_wu, ffn_wd=ffn_wd, w_in=w_in, w_out=w_out,
             rw_mu=rw_mu, rw_w0=rw_w0, rw_w2=rw_w2, rw_a0=rw_a0, rw_a2=rw_a2, rw_g2=rw_g2, rw_k_k=rw_k_k,
             rw_k_a=rw_k_a, rw_r_k=rw_r_k.reshape(DEPTH, RW_W), rw_gn_g=rw_gn_g, rw_gn_b=rw_gn_b,
             gla_a2=gla_a2, gla_a_b=gla_a_b, gla_gn_g=gla_gn_g, ml_conv_w=ml_conv_w, ml_conv_b=ml_conv_b,
             ml_i_b=ml_i_b, ml_f_b=ml_f_b, ml_gn_g=ml_gn_g)
    layers = [_prep_layer(l, W) for l in range(DEPTH)]
    Bp, Tp, _ = x_prompt.shape
    Bs, Ts, _ = x_sample.shape

    mods = _ada_call(jnp.concatenate([c_prompt, c_sample], axis=0), ada_w, ada_b)
    mods_p = mods[:, :Bp].reshape(DEPTH, Bp, 9, 1, D_MODEL)
    mods_s = mods[:, Bp:].reshape(DEPTH, Bs, 9, 1, D_MODEL)

    dt = x_prompt.dtype
    zero_states = (
        jnp.zeros((DEPTH, Bp, RW_HEADS, RW_HD, RW_HD), dt),
        jnp.zeros((DEPTH, Bp, RW_PROJ), dt),
        jnp.zeros((DEPTH, Bp, GLA_HEADS, GLA_DK, GLA_DV), dt),
        jnp.zeros((DEPTH, Bp, ML_HEADS, ML_HD, ML_HD), dt),
        jnp.zeros((DEPTH, Bp, ML_HEADS, ML_HD), dt),
        jnp.zeros((DEPTH, Bp, ML_HEADS), dt),
        jnp.zeros((DEPTH, Bp, ML_CONV - 1, 2 * ML_W), dt),
    )
    y_prompt, p_states = _run_trunk(x_prompt, mods_p, zero_states, layers, 1, 512, (2, 2, 1))
    sample_states = (state_rwkv, state_rwkv_shift, state_gla, state_mlstm_c, state_mlstm_n,
                     state_mlstm_m, state_mlstm_conv)
    y_sample, s_states = _run_trunk(x_sample, mods_s, sample_states, layers, 512 // Ts, Ts, (16, 16, 16))
    return (y_prompt, y_sample) + p_states + s_states
```
